```python
import jax, jax.numpy as jnp
from jax import lax
import numpy as np

D_MODEL = 2048
BATCH = 4
SEQ = 2048
DEPTH = 4

CHUNK = 64
N_HEADS = 16
HEAD_DIM = D_MODEL // N_HEADS
D_FF = 5632
BLOCK_Q = 128
LEFT_CHUNKS = 8
BAND = LEFT_CHUNKS + 1
REL_CLIP = 256
N_A = DEPTH // 2
N_B = DEPTH - N_A
EPS = 1e-6

kernel_name = "yoco_stickbreak_chunkband_macaron"


def rms_norm(x, g):
    x32 = x.astype(jnp.float32)
    y = x32 * lax.rsqrt(jnp.mean(x32 * x32, axis=-1, keepdims=True) + EPS)
    return (y * g.astype(jnp.float32)).astype(x.dtype)


def swiglu(x, w_gate, w_up, w_down):
    return (jax.nn.silu(x @ w_gate) * (x @ w_up)) @ w_down


def stick_breaking_attention(q, k, v):
    seq = q.shape[1]
    scale = HEAD_DIM ** -0.5
    outs = []
    for i in range(seq // BLOCK_Q):
        q0 = i * BLOCK_Q
        kl = q0 + BLOCK_Q
        z = jnp.einsum('bqhd,bkhd->bhqk', q[:, q0:kl], k[:, :kl]).astype(jnp.float32) * scale
        t_pos = q0 + jnp.arange(BLOCK_Q)
        s_pos = jnp.arange(kl)
        mask = s_pos[None, :] < t_pos[:, None]
        log_1m = jnp.where(mask, jax.nn.log_sigmoid(-z), 0.0)
        log_a = jax.nn.log_sigmoid(z) + lax.cumsum(log_1m, axis=3, reverse=True) - log_1m
        a = jnp.where(mask, jnp.exp(log_a), 0.0)
        outs.append(jnp.einsum('bhqk,bkhd->bqhd', a.astype(v.dtype), v[:, :kl]))
    out = jnp.concatenate(outs, axis=1)
    return out.reshape(out.shape[0], seq, N_HEADS * HEAD_DIM)


def chunked_band_attention(q, k_pad, v_pad, rel_bias):
    b, seq, h, dh = q.shape
    nc = seq // CHUNK
    scale = HEAD_DIM ** -0.5
    qc = q.reshape(b, nc, CHUNK, h, dh)
    kc = k_pad.reshape(b, nc + LEFT_CHUNKS, CHUNK, h, dh)
    vc = v_pad.reshape(b, nc + LEFT_CHUNKS, CHUNK, h, dh)
    scores = jnp.concatenate(
        [jnp.einsum('bnqhd,bnkhd->bhnqk', qc, kc[:, j:j + nc]) for j in range(BAND)],
        axis=-1).astype(jnp.float32) * scale
    i_pos = jnp.arange(CHUNK)
    p_pos = jnp.arange(BAND * CHUNK)
    rel = LEFT_CHUNKS * CHUNK + i_pos[:, None] - p_pos[None, :]
    idx = jnp.clip(rel, -REL_CLIP, REL_CLIP) + REL_CLIP
    bias = rel_bias[:, idx].astype(jnp.float32)
    c_idx = jnp.arange(nc)
    valid = (c_idx[:, None] - LEFT_CHUNKS + p_pos[None, :] // CHUNK) >= 0
    scores = scores + bias[None, :, None]
    scores = jnp.where(valid[None, None, :, None, :], scores, -jnp.inf)
    probs = jax.nn.softmax(scores, axis=-1).astype(v_pad.dtype)
    out = jnp.einsum('bhnqk,bnkhd->bnqhd', probs[..., :CHUNK], vc[:, 0:nc])
    for j in range(1, BAND):
        out = out + jnp.einsum('bhnqk,bnkhd->bnqhd',
                               probs[..., j * CHUNK:(j + 1) * CHUNK], vc[:, j:j + nc])
    return out.reshape(b, seq, h * dh)


def setup_inputs(seed: int = 0) -> dict:
    key = jax.random.key(seed)
    ks = jax.random.split(key, 16)
    f32 = jnp.float32
    d, f = D_MODEL, D_FF
    x = jax.random.normal(ks[0], (BATCH, SEQ, d), f32)
    g_ffn = 1.0 + 0.05 * jax.random.normal(ks[1], (DEPTH, 2, d), f32)
    w_ffn_gate = jax.random.normal(ks[2], (DEPTH, 2, d, f), f32) * d ** -0.5
    w_ffn_up = jax.random.normal(ks[3], (DEPTH, 2, d, f), f32) * d ** -0.5
    w_ffn_down = jax.random.normal(ks[4], (DEPTH, 2, f, d), f32) * f ** -0.5
    g_mix = 1.0 + 0.05 * jax.random.normal(ks[5], (DEPTH, d), f32)
    w_qkv_a = jax.random.normal(ks[6], (N_A, d, 3 * d), f32) * d ** -0.5
    w_o_a = jax.random.normal(ks[7], (N_A, d, d), f32) * d ** -0.5
    g_kv = 1.0 + 0.05 * jax.random.normal(ks[8], (d,), f32)
    w_kv_shared = jax.random.normal(ks[9], (d, 2 * d), f32) * d ** -0.5
    w_q_b = jax.random.normal(ks[10], (N_B, d, d), f32) * d ** -0.5
    w_o_b = jax.random.normal(ks[11], (N_B, d, d), f32) * d ** -0.5
    rel_bias_b = 0.1 * jax.random.normal(ks[12], (N_B, N_HEADS, 2 * REL_CLIP + 1), f32)
    g_final = 1.0 + 0.05 * jax.random.normal(ks[13], (d,), f32)
    return {"x": x, "g_ffn": g_ffn, "w_ffn_gate": w_ffn_gate, "w_ffn_up": w_ffn_up,
            "w_ffn_down": w_ffn_down, "g_mix": g_mix, "w_qkv_a": w_qkv_a, "w_o_a": w_o_a,
            "g_kv": g_kv, "w_kv_shared": w_kv_shared, "w_q_b": w_q_b, "w_o_b": w_o_b,
            "rel_bias_b": rel_bias_b, "g_final": g_final}


def reference(x, g_ffn, w_ffn_gate, w_ffn_up, w_ffn_down, g_mix, w_qkv_a, w_o_a,
              g_kv, w_kv_shared, w_q_b, w_o_b, rel_bias_b, g_final):
    b, seq, d = x.shape
    h = x
    k_pad = None
    v_pad = None
    for layer in range(DEPTH):
        h = h + 0.5 * swiglu(rms_norm(h, g_ffn[layer, 0]), w_ffn_gate[layer, 0],
                             w_ffn_up[layer, 0], w_ffn_down[layer, 0])
        if layer < N_A:
            u = rms_norm(h, g_mix[layer])
            qkv = (u @ w_qkv_a[layer]).reshape(b, seq, 3, N_HEADS, HEAD_DIM)
            mix = stick_breaking_attention(qkv[:, :, 0], qkv[:, :, 1], qkv[:, :, 2])
            h = h + mix @ w_o_a[layer]
        else:
            if layer == N_A:
                kv = (rms_norm(h, g_kv) @ w_kv_shared).reshape(b, seq, 2, N_HEADS, HEAD_DIM)
                pad = ((0, 0), (LEFT_CHUNKS * CHUNK, 0), (0, 0), (0, 0))
                k_pad = jnp.pad(kv[:, :, 0], pad)
                v_pad = jnp.pad(kv[:, :, 1], pad)
            lb = layer - N_A
            u = rms_norm(h, g_mix[layer])
            q = (u @ w_q_b[lb]).reshape(b, seq, N_HEADS, HEAD_DIM)
            mix = chunked_band_attention(q, k_pad, v_pad, rel_bias_b[lb])
            h = h + mix @ w_o_b[lb]
        h = h + 0.5 * swiglu(rms_norm(h, g_ffn[layer, 1]), w_ffn_gate[layer, 1],
                             w_ffn_up[layer, 1], w_ffn_down[layer, 1])
    return rms_norm(h, g_final)
```

```python
import functools

import jax
import jax.numpy as jnp
from jax import lax
from jax.experimental import pallas as pl
from jax.experimental.pallas import tpu as pltpu

D_MODEL = 2048
N_HEADS = 16
HEAD_DIM = D_MODEL // N_HEADS
D_FF = 5632
CHUNK = 64
LEFT_CHUNKS = 8
REL_CLIP = 256
EPS = 1e-6
SCALE = HEAD_DIM ** -0.5

BF16 = jnp.bfloat16
F32 = jnp.float32

VMEM_LIMIT_BYTES = 56 * 1024 * 1024

TILE_M = 1024
TILE_N = 1024
TILE_F = 512
FFN_DOWN_COLS = 512

SB_TQ = 256
SB_KCHUNK = 512
SB_KSUB = 128

BAND_TQ = 256
BAND_WIN = BAND_TQ + LEFT_CHUNKS * CHUNK
BAND_SHIFTS = LEFT_CHUNKS * CHUNK // BAND_TQ + 1
TABLE_WIN = (LEFT_CHUNKS + 2) * CHUNK
BIAS_ROWS = 8


def _params(*semantics):
    return pltpu.CompilerParams(dimension_semantics=semantics,
                                vmem_limit_bytes=VMEM_LIMIT_BYTES)


def _rms_norm_f32(x, g):
    y = x * lax.rsqrt(jnp.mean(x * x, axis=-1, keepdims=True) + EPS)
    return y * g


def _norm_matmul_kernel(x_ref, g_ref, w_ref, o_ref, xn_ref):
    @pl.when(pl.program_id(1) == 0)
    def _():
        xn_ref[...] = _rms_norm_f32(x_ref[...], g_ref[...]).astype(BF16)

    o_ref[...] = jnp.dot(xn_ref[...], w_ref[...],
                         preferred_element_type=F32).astype(o_ref.dtype)


def _norm_matmul(x, g, w):
    m, d = x.shape
    n = w.shape[1]
    return pl.pallas_call(
        _norm_matmul_kernel,
        grid=(m // TILE_M, n // TILE_N),
        in_specs=[pl.BlockSpec((TILE_M, d), lambda i, j: (i, 0)),
                  pl.BlockSpec((1, d), lambda i, j: (0, 0)),
                  pl.BlockSpec((d, TILE_N), lambda i, j: (0, j))],
        out_specs=pl.BlockSpec((TILE_M, TILE_N), lambda i, j: (i, j)),
        out_shape=jax.ShapeDtypeStruct((m, n), BF16),
        scratch_shapes=[pltpu.VMEM((TILE_M, d), BF16)],
        compiler_params=_params("parallel", "arbitrary"),
        name="norm_matmul",
    )(x, g.reshape(1, d), w)


def _matmul_residual_kernel(a_ref, w_ref, r_ref, o_ref):
    o_ref[...] = r_ref[...] + jnp.dot(a_ref[...], w_ref[...], preferred_element_type=F32)


def _matmul_residual(a, w, res):
    m, k = a.shape
    n = w.shape[1]
    return pl.pallas_call(
        _matmul_residual_kernel,
        grid=(m // TILE_M, n // TILE_N),
        in_specs=[pl.BlockSpec((TILE_M, k), lambda i, j: (i, 0)),
                  pl.BlockSpec((k, TILE_N), lambda i, j: (0, j)),
                  pl.BlockSpec((TILE_M, TILE_N), lambda i, j: (i, j))],
        out_specs=pl.BlockSpec((TILE_M, TILE_N), lambda i, j: (i, j)),
        out_shape=jax.ShapeDtypeStruct((m, n), F32),
        compiler_params=_params("parallel", "parallel"),
        name="matmul_residual",
    )(a, w, res)


def _ffn_kernel(h_ref, g_ref, wg_ref, wu_ref, wd_ref, o_ref, xn_ref):
    j = pl.program_id(1)

    @pl.when(j == 0)
    def _():
        xn_ref[...] = _rms_norm_f32(h_ref[...], g_ref[...]).astype(BF16)
        o_ref[...] = jnp.zeros_like(o_ref)

    xn = xn_ref[...]
    gate = jnp.dot(xn, wg_ref[...], preferred_element_type=F32)
    up = jnp.dot(xn, wu_ref[...], preferred_element_type=F32)
    hidden = (gate * (1.0 / (1.0 + jnp.exp(-gate))) * up).astype(BF16)
    for n0 in range(0, o_ref.shape[1], FFN_DOWN_COLS):
        cols = slice(n0, n0 + FFN_DOWN_COLS)
        o_ref[:, cols] += jnp.dot(hidden, wd_ref[:, cols], preferred_element_type=F32)

    @pl.when(j == pl.num_programs(1) - 1)
    def _():
        o_ref[...] = h_ref[...] + 0.5 * o_ref[...]


def _ffn_half_step(h, g, w_gate, w_up, w_down):
    m, d = h.shape
    f = w_gate.shape[1]
    return pl.pallas_call(
        _ffn_kernel,
        grid=(m // TILE_M, f // TILE_F),
        in_specs=[pl.BlockSpec((TILE_M, d), lambda i, j: (i, 0),
                               pipeline_mode=pl.Buffered(1)),
                  pl.BlockSpec((1, d), lambda i, j: (0, 0)),
                  pl.BlockSpec((d, TILE_F), lambda i, j: (0, j)),
                  pl.BlockSpec((d, TILE_F), lambda i, j: (0, j)),
                  pl.BlockSpec((TILE_F, d), lambda i, j: (j, 0))],
        out_specs=pl.BlockSpec((TILE_M, d), lambda i, j: (i, 0)),
        out_shape=jax.ShapeDtypeStruct((m, d), F32),
        scratch_shapes=[pltpu.VMEM((TILE_M, d), BF16)],
        compiler_params=_params("parallel", "arbitrary"),
        name="ffn_half_step",
    )(h, g.reshape(1, d), w_gate, w_up, w_down)


def _stick_breaking_kernel(q_ref, k_ref, v_ref, o_ref, acc_ref, csum_ref):
    i = pl.program_id(2)
    q = q_ref[...]
    t_pos = i * SB_TQ + lax.broadcasted_iota(jnp.int32, (SB_TQ, SB_KSUB), 0)
    lane = lax.broadcasted_iota(jnp.int32, (SB_TQ, SB_KSUB), 1)

    row = lax.broadcasted_iota(jnp.int32, (2 * SB_KSUB, 2 * SB_KSUB), 0) & (SB_KSUB - 1)
    col = lax.broadcasted_iota(jnp.int32, (2 * SB_KSUB, 2 * SB_KSUB), 1)
    suffix_mat = jnp.where((col >= SB_KSUB) | (row > col), 1.0, 0.0).astype(BF16)

    acc_ref[...] = jnp.zeros_like(acc_ref)
    csum_ref[...] = jnp.zeros_like(csum_ref)
    n_chunks = (i * SB_TQ + SB_TQ + SB_KCHUNK - 1) // SB_KCHUNK

    def chunk_body(n, carry):
        k0 = pl.multiple_of((n_chunks - 1 - n) * SB_KCHUNK, SB_KCHUNK)
        csum = csum_ref[...]
        acc = acc_ref[...]
        for sub in reversed(range(SB_KCHUNK // SB_KSUB)):
            start = pl.multiple_of(k0 + sub * SB_KSUB, SB_KSUB)
            ks = k_ref[pl.ds(start, SB_KSUB), :]
            vs = v_ref[pl.ds(start, SB_KSUB), :]
            z = lax.dot_general(q, ks, (((1,), (1,)), ((), ())),
                                preferred_element_type=F32) * SCALE
            mask = (start + lane) < t_pos
            softplus = jnp.maximum(z, 0.0) + jnp.log(1.0 + jnp.exp(-jnp.abs(z)))
            log_1m = jnp.where(mask, -softplus, 0.0)
            hi = log_1m.astype(BF16)
            lo = (log_1m - hi.astype(F32)).astype(BF16)
            sums = jnp.dot(jnp.concatenate([hi, lo], axis=1), suffix_mat,
                           preferred_element_type=F32)
            log_a = (z - softplus) + sums[:, :SB_KSUB] + csum
            a = jnp.where(mask, jnp.exp(log_a), 0.0)
            acc = acc + jnp.dot(a.astype(BF16), vs, preferred_element_type=F32)
            csum = csum + sums[:, SB_KSUB:]
        csum_ref[...] = csum
        acc_ref[...] = acc
        return carry

    lax.fori_loop(0, n_chunks, chunk_body, 0)
    o_ref[...] = acc_ref[...].astype(o_ref.dtype)


def _stick_breaking_attention(qkv, batch, seq):
    nq = seq // SB_TQ
    return pl.pallas_call(
        _stick_breaking_kernel,
        grid=(batch, N_HEADS, nq),
        in_specs=[pl.BlockSpec((SB_TQ, HEAD_DIM), lambda b, h, i: (b * nq + i, h)),
                  pl.BlockSpec((seq, HEAD_DIM), lambda b, h, i: (b, N_HEADS + h)),
                  pl.BlockSpec((seq, HEAD_DIM), lambda b, h, i: (b, 2 * N_HEADS + h))],
        out_specs=pl.BlockSpec((SB_TQ, HEAD_DIM), lambda b, h, i: (b * nq + i, h)),
        out_shape=jax.ShapeDtypeStruct((batch * seq, D_MODEL), BF16),
        scratch_shapes=[pltpu.VMEM((SB_TQ, HEAD_DIM), F32),
                        pltpu.VMEM((SB_TQ, SB_KSUB), F32)],
        compiler_params=_params("parallel", "parallel", "parallel"),
        name="stick_breaking_attention",
    )(qkv, qkv, qkv)


def _band_bias_kernel(t0_ref, t1_ref, t2_ref, o_ref):
    u0 = pl.program_id(0) * BIAS_ROWS
    entry = lax.broadcasted_iota(jnp.int32, (TABLE_WIN, BAND_WIN), 0)
    c = lax.broadcasted_iota(jnp.int32, (TABLE_WIN, BAND_WIN), 1)
    entry_plus_c = entry + c
    c_chunk = lax.shift_right_logical(
        lax.broadcasted_iota(jnp.int32, (N_HEADS, BAND_WIN), 1), CHUNK.bit_length() - 1)
    for r in range(BIAS_ROWS):
        u = u0 + r
        onehot = jnp.where(entry_plus_c == u + CHUNK, 1.0, 0.0).astype(BF16)
        bias = (jnp.dot(t0_ref[...], onehot, preferred_element_type=F32)
                + jnp.dot(t1_ref[...], onehot, preferred_element_type=F32)
                + jnp.dot(t2_ref[...], onehot, preferred_element_type=F32))
        chunk_gap = u // CHUNK - c_chunk
        valid = (chunk_gap >= 0) & (chunk_gap <= LEFT_CHUNKS)
        o_ref[r] = jnp.where(valid, bias, -jnp.inf)


def _band_bias(rel_bias):
    lo = REL_CLIP - CHUNK
    n_edge = TABLE_WIN - (rel_bias.shape[1] - lo)
    window = jnp.concatenate(
        [rel_bias[:, lo:], jnp.broadcast_to(rel_bias[:, -1:], (N_HEADS, n_edge))], axis=1)
    t0 = window.astype(BF16)
    rem = window - t0.astype(F32)
    t1 = rem.astype(BF16)
    t2 = (rem - t1.astype(F32)).astype(BF16)
    table_spec = pl.BlockSpec((N_HEADS, TABLE_WIN), lambda s: (0, 0))
    out = pl.pallas_call(
        _band_bias_kernel,
        grid=(BAND_WIN // BIAS_ROWS,),
        in_specs=[table_spec, table_spec, table_spec],
        out_specs=pl.BlockSpec((BIAS_ROWS, N_HEADS, BAND_WIN), lambda s: (s, 0, 0)),
        out_shape=jax.ShapeDtypeStruct((BAND_WIN, N_HEADS, BAND_WIN), F32),
        compiler_params=_params("parallel"),
        name="band_bias",
    )(t0, t1, t2)
    return out.transpose(1, 0, 2)


def _band_attention_kernel(q_ref, k_ref, v_ref, bias_ref, o_ref):
    i = pl.program_id(2)
    start = pl.multiple_of(jnp.maximum(i * BAND_TQ - LEFT_CHUNKS * CHUNK, 0), BAND_TQ)
    k = k_ref[pl.ds(start, BAND_WIN), :]
    v = v_ref[pl.ds(start, BAND_WIN), :]
    scores = lax.dot_general(q_ref[...], k, (((1,), (1,)), ((), ())),
                             preferred_element_type=F32) * SCALE + bias_ref[0]
    m = jnp.max(scores, axis=-1, keepdims=True)
    p = jnp.exp(scores - m)
    probs = p * (1.0 / jnp.sum(p, axis=-1, keepdims=True))
    o_ref[...] = jnp.dot(probs.astype(BF16), v, preferred_element_type=F32).astype(o_ref.dtype)


def _band_attention(q, kv, bias, batch, seq):
    nq = seq // BAND_TQ
    return pl.pallas_call(
        _band_attention_kernel,
        grid=(batch, N_HEADS, nq),
        in_specs=[pl.BlockSpec((BAND_TQ, HEAD_DIM), lambda b, h, i: (b * nq + i, h)),
                  pl.BlockSpec((seq, HEAD_DIM), lambda b, h, i: (b, h)),
                  pl.BlockSpec((seq, HEAD_DIM), lambda b, h, i: (b, N_HEADS + h)),
                  pl.BlockSpec((1, BAND_TQ, BAND_WIN),
                               lambda b, h, i: (h, jnp.minimum(i, BAND_SHIFTS - 1), 0))],
        out_specs=pl.BlockSpec((BAND_TQ, HEAD_DIM), lambda b, h, i: (b * nq + i, h)),
        out_shape=jax.ShapeDtypeStruct((batch * seq, D_MODEL), BF16),
        compiler_params=_params("parallel", "parallel", "parallel"),
        name="band_attention",
    )(q, kv, kv, bias)


def _rms_norm_kernel(x_ref, g_ref, o_ref):
    o_ref[...] = _rms_norm_f32(x_ref[...], g_ref[...])


def _rms_norm(x, g):
    m, d = x.shape
    return pl.pallas_call(
        _rms_norm_kernel,
        grid=(m // TILE_M,),
        in_specs=[pl.BlockSpec((TILE_M, d), lambda i: (i, 0)),
                  pl.BlockSpec((1, d), lambda i: (0, 0))],
        out_specs=pl.BlockSpec((TILE_M, d), lambda i: (i, 0)),
        out_shape=jax.ShapeDtypeStruct((m, d), F32),
        compiler_params=_params("parallel"),
        name="final_rms_norm",
    )(x, g.reshape(1, d))


def kernel(x, g_ffn, w_ffn_gate, w_ffn_up, w_ffn_down, g_mix, w_qkv_a, w_o_a, g_kv,
           w_kv_shared, w_q_b, w_o_b, rel_bias_b, g_final):
    batch, seq, d = x.shape
    depth = g_ffn.shape[0]
    n_a = w_qkv_a.shape[0]

    w_gate, w_up, w_down = (w.astype(BF16) for w in (w_ffn_gate, w_ffn_up, w_ffn_down))
    w_qkv, w_oa, w_kv, w_qb, w_ob = (
        w.astype(BF16) for w in (w_qkv_a, w_o_a, w_kv_shared, w_q_b, w_o_b))

    h = x.reshape(batch * seq, d)
    kv = None
    for layer in range(depth):
        h = _ffn_half_step(h, g_ffn[layer, 0], w_gate[layer, 0], w_up[layer, 0],
                           w_down[layer, 0])
        if layer < n_a:
            qkv = _norm_matmul(h, g_mix[layer], w_qkv[layer])
            mix = _stick_breaking_attention(qkv, batch, seq)
            h = _matmul_residual(mix, w_oa[layer], h)
        else:
            lb = layer - n_a
            if kv is None:
                kv = _norm_matmul(h, g_kv, w_kv)
            q = _norm_matmul(h, g_mix[layer], w_qb[lb])
            mix = _band_attention(q, kv, _band_bias(rel_bias_b[lb]), batch, seq)
            h = _matmul_residual(mix, w_ob[lb], h)
        h = _ffn_half_step(h, g_ffn[layer, 1], w_gate[layer, 1], w_up[layer, 1],
                           w_down[layer, 1])
    return _rms_norm(h, g_final).reshape(batch, seq, d)
```

```python
import math

import jax
import jax.numpy as jnp
from jax import lax
from jax.experimental import pallas as pl
from jax.experimental.pallas import tpu as pltpu

D_MODEL = 2048
N_HEADS = 16
HEAD_DIM = D_MODEL // N_HEADS
CHUNK = 64
LEFT_CHUNKS = 8
REL_CLIP = 256
EPS = 1e-6
SCALE = HEAD_DIM ** -0.5
LOG2E = math.log2(math.e)

BF16 = jnp.bfloat16
F32 = jnp.float32

VMEM_LIMIT_BYTES = 56 * 1024 * 1024

TILE_M = 1024
TILE_N = 1024
TILE_F = 512
FFN_DOWN_COLS = 512

SB_TQ = 256
SB_KCHUNK = 2 * SB_TQ
SB_KSUB = 128

BAND_TQ = 256
BAND_WIN = BAND_TQ + LEFT_CHUNKS * CHUNK
BAND_SHIFTS = LEFT_CHUNKS * CHUNK // BAND_TQ + 1
BIAS_ROWS = 8

assert REL_CLIP <= BAND_WIN and BAND_WIN % 128 == 0


def _params(*semantics):
    return pltpu.CompilerParams(dimension_semantics=semantics,
                                vmem_limit_bytes=VMEM_LIMIT_BYTES)


def _stacked_spec(lead, block, index_map):
    lead = tuple(lead)
    return pl.BlockSpec((None,) * len(lead) + tuple(block),
                        lambda *grid_ids: lead + tuple(index_map(*grid_ids)))


def _rms_norm_f32(x, g):
    y = x * lax.rsqrt(jnp.mean(x * x, axis=-1, keepdims=True) + EPS)
    return y * g


def _norm_matmul_kernel(x_ref, g_ref, w_ref, o_ref, xn_ref):
    @pl.when(pl.program_id(1) == 0)
    def _():
        xn_ref[...] = _rms_norm_f32(x_ref[...], g_ref[...]).astype(BF16)

    o_ref[...] = jnp.dot(xn_ref[...], w_ref[...].astype(BF16),
                         preferred_element_type=F32).astype(o_ref.dtype)


def _norm_matmul(x, g, w, lead=()):
    m, d = x.shape
    n = w.shape[-1]
    return pl.pallas_call(
        _norm_matmul_kernel,
        grid=(m // TILE_M, n // TILE_N),
        in_specs=[pl.BlockSpec((TILE_M, d), lambda i, j: (i, 0)),
                  pl.BlockSpec((1, d), lambda i, j: (0, 0)),
                  _stacked_spec(lead, (d, TILE_N), lambda i, j: (0, j))],
        out_specs=pl.BlockSpec((TILE_M, TILE_N), lambda i, j: (i, j)),
        out_shape=jax.ShapeDtypeStruct((m, n), BF16),
        scratch_shapes=[pltpu.VMEM((TILE_M, d), BF16)],
        compiler_params=_params("parallel", "arbitrary"),
        name="norm_matmul",
    )(x, g.reshape(1, d), w)


def _matmul_residual_kernel(a_ref, w_ref, r_ref, o_ref):
    o_ref[...] = r_ref[...] + jnp.dot(a_ref[...], w_ref[...].astype(BF16),
                                      preferred_element_type=F32)


def _matmul_residual(a, w, res, lead=()):
    m, k = a.shape
    n = w.shape[-1]
    return pl.pallas_call(
        _matmul_residual_kernel,
        grid=(m // TILE_M, n // TILE_N),
        in_specs=[pl.BlockSpec((TILE_M, k), lambda i, j: (i, 0)),
                  _stacked_spec(lead, (k, TILE_N), lambda i, j: (0, j)),
                  pl.BlockSpec((TILE_M, TILE_N), lambda i, j: (i, j))],
        out_specs=pl.BlockSpec((TILE_M, TILE_N), lambda i, j: (i, j)),
        out_shape=jax.ShapeDtypeStruct((m, n), F32),
        compiler_params=_params("parallel", "parallel"),
        name="matmul_residual",
    )(a, w, res)


def _ffn_kernel(h_ref, g_ref, wg_ref, wu_ref, wd_ref, o_ref, xn_ref):
    j = pl.program_id(1)

    @pl.when(j == 0)
    def _():
        xn_ref[...] = _rms_norm_f32(h_ref[...], g_ref[...]).astype(BF16)
        o_ref[...] = jnp.zeros_like(o_ref)

    xn = xn_ref[...]
    gate = jnp.dot(xn, wg_ref[...], preferred_element_type=F32)
    up = jnp.dot(xn, wu_ref[...], preferred_element_type=F32)
    hidden = (gate * (1.0 / (1.0 + jnp.exp(-gate))) * up).astype(BF16)
    for n0 in range(0, o_ref.shape[1], FFN_DOWN_COLS):
        cols = slice(n0, n0 + FFN_DOWN_COLS)
        o_ref[:, cols] += jnp.dot(hidden, wd_ref[:, cols], preferred_element_type=F32)

    @pl.when(j == pl.num_programs(1) - 1)
    def _():
        o_ref[...] = h_ref[...] + 0.5 * o_ref[...]


def _ffn_half_step(h, g, w_gate, w_up, w_down, lead):
    m, d = h.shape
    f = w_gate.shape[-1]
    return pl.pallas_call(
        _ffn_kernel,
        grid=(m // TILE_M, f // TILE_F),
        in_specs=[pl.BlockSpec((TILE_M, d), lambda i, j: (i, 0),
                               pipeline_mode=pl.Buffered(1)),
                  pl.BlockSpec((1, d), lambda i, j: (0, 0)),
                  _stacked_spec(lead, (d, TILE_F), lambda i, j: (0, j)),
                  _stacked_spec(lead, (d, TILE_F), lambda i, j: (0, j)),
                  _stacked_spec(lead, (TILE_F, d), lambda i, j: (j, 0))],
        out_specs=pl.BlockSpec((TILE_M, d), lambda i, j: (i, 0)),
        out_shape=jax.ShapeDtypeStruct((m, d), F32),
        scratch_shapes=[pltpu.VMEM((TILE_M, d), BF16)],
        compiler_params=_params("parallel", "arbitrary"),
        name="ffn_half_step",
    )(h, g.reshape(1, d), w_gate, w_up, w_down)


def _sb_key_block(q, k_ref, v_ref, start, width, mask, suffix_mat, acc_ref, csum_ref):
    ks = k_ref[pl.ds(start, width), :]
    vs = v_ref[pl.ds(start, width), :]
    w = lax.dot_general(q, ks, (((1,), (1,)), ((), ())),
                        preferred_element_type=F32) * (SCALE * LOG2E)
    sign_bit = jnp.uint32(0x80000000)
    neg_abs = lax.bitcast_convert_type(lax.bitcast_convert_type(w, jnp.uint32) | sign_bit, F32)
    sp = jnp.maximum(w, 0.0) + jnp.log2(1.0 + jnp.exp2(neg_abs))
    sp_sum = sp if mask is None else jnp.where(mask, sp, 0.0)

    csum = csum_ref[...]
    tails = []
    for sub in reversed(range(width // SB_KSUB)):
        blk = sp_sum[:, sub * SB_KSUB:(sub + 1) * SB_KSUB]
        hi = blk.astype(BF16)
        lo = (blk - hi.astype(F32)).astype(BF16)
        sums = jnp.dot(jnp.concatenate([hi, lo], axis=1), suffix_mat,
                       preferred_element_type=F32)
        tails.append(sums[:, :SB_KSUB] + csum)
        csum = csum + sums[:, SB_KSUB:]
    csum_ref[...] = csum
    tail = jnp.concatenate(tails[::-1], axis=1)

    a = jnp.exp2((w - sp) - tail)
    if mask is not None:
        a = jnp.where(mask, a, 0.0)
    acc_ref[...] += jnp.dot(a.astype(BF16), vs, preferred_element_type=F32)


def _stick_breaking_kernel(q_ref, k_ref, v_ref, o_ref, acc_ref, csum_ref):
    seq = q_ref.shape[0]

    row = lax.broadcasted_iota(jnp.int32, (2 * SB_KSUB, 2 * SB_KSUB), 0) & (SB_KSUB - 1)
    col = lax.broadcasted_iota(jnp.int32, (2 * SB_KSUB, 2 * SB_KSUB), 1)
    suffix_mat = jnp.where((col >= SB_KSUB) | (row > col), 1.0, 0.0).astype(BF16)

    def tile(base, row_offset, n_full):
        t0 = pl.multiple_of(base + row_offset, SB_TQ)
        diag_width = row_offset + SB_TQ
        q = q_ref[pl.ds(t0, SB_TQ), :]
        acc_ref[...] = jnp.zeros_like(acc_ref)
        csum_ref[...] = jnp.zeros_like(csum_ref)
        r = lax.broadcasted_iota(jnp.int32, (SB_TQ, diag_width), 0)
        c = lax.broadcasted_iota(jnp.int32, (SB_TQ, diag_width), 1)
        _sb_key_block(q, k_ref, v_ref, base, diag_width, c < r + row_offset, suffix_mat,
                      acc_ref, csum_ref)

        def full_chunk(n, carry):
            start = pl.multiple_of((n_full - 1 - n) * SB_KCHUNK, SB_KCHUNK)
            _sb_key_block(q, k_ref, v_ref, start, SB_KCHUNK, None, suffix_mat, acc_ref, csum_ref)
            return carry

        lax.fori_loop(0, n_full, full_chunk, 0)
        o_ref[pl.ds(t0, SB_TQ), :] = acc_ref[...].astype(o_ref.dtype)

    def tile_pair(p, carry):
        base = pl.multiple_of(p * SB_KCHUNK, SB_KCHUNK)
        tile(base, 0, p)
        tile(base, SB_TQ, p)
        return carry

    lax.fori_loop(0, seq // SB_KCHUNK, tile_pair, 0)


def _stick_breaking_attention(qkv, batch, seq):
    head_block = lambda col0: pl.BlockSpec((seq, HEAD_DIM), lambda b, h: (b, col0 + h))
    return pl.pallas_call(
        _stick_breaking_kernel,
        grid=(batch, N_HEADS),
        in_specs=[head_block(0), head_block(N_HEADS), head_block(2 * N_HEADS)],
        out_specs=head_block(0),
        out_shape=jax.ShapeDtypeStruct((batch * seq, D_MODEL), BF16),
        scratch_shapes=[pltpu.VMEM((SB_TQ, HEAD_DIM), F32),
                        pltpu.VMEM((SB_TQ, SB_KSUB), F32)],
        compiler_params=_params("parallel", "parallel"),
        name="stick_breaking_attention",
    )(qkv, qkv, qkv)


def _band_bias_kernel(y_ref, o_ref):
    y = jnp.broadcast_to(y_ref[0], (BIAS_ROWS, 2 * BAND_WIN))
    c_chunk = lax.shift_right_logical(
        lax.broadcasted_iota(jnp.int32, (BIAS_ROWS, BAND_WIN), 1), CHUNK.bit_length() - 1)

    def rows(g, carry):
        u0 = pl.multiple_of(g * BIAS_ROWS, BIAS_ROWS)
        rolled = pltpu.roll(y, u0, 1, stride=1, stride_axis=0)
        chunk_gap = lax.shift_right_logical(u0, CHUNK.bit_length() - 1) - c_chunk
        valid = (chunk_gap >= 0) & (chunk_gap <= LEFT_CHUNKS)
        o_ref[0, pl.ds(u0, BIAS_ROWS), :] = jnp.where(valid, rolled[:, BAND_WIN:] * LOG2E,
                                                      -jnp.inf)
        return carry

    lax.fori_loop(0, BAND_WIN // BIAS_ROWS, rows, 0)


def _band_bias(rel_bias):
    edge = BAND_WIN - REL_CLIP
    y = jnp.concatenate(
        [jnp.broadcast_to(rel_bias[:, -1:], (N_HEADS, edge)), rel_bias[:, ::-1],
         jnp.broadcast_to(rel_bias[:, :1], (N_HEADS, edge - 1))], axis=1)
    return pl.pallas_call(
        _band_bias_kernel,
        grid=(N_HEADS,),
        in_specs=[pl.BlockSpec((1, 1, 2 * BAND_WIN), lambda h: (h, 0, 0))],
        out_specs=pl.BlockSpec((1, BAND_WIN, BAND_WIN), lambda h: (h, 0, 0)),
        out_shape=jax.ShapeDtypeStruct((N_HEADS, BAND_WIN, BAND_WIN), F32),
        compiler_params=_params("parallel"),
        name="band_bias",
    )(y.reshape(N_HEADS, 1, 2 * BAND_WIN))


def _band_attention_kernel(q_ref, k_ref, v_ref, bias_ref, o_ref):
    n_tiles = q_ref.shape[0] // BAND_TQ

    def tile(i, carry):
        t0 = pl.multiple_of(i * BAND_TQ, BAND_TQ)
        start = pl.multiple_of(jnp.maximum(t0 - LEFT_CHUNKS * CHUNK, 0), BAND_TQ)
        bias_row = pl.multiple_of(t0 - start, BAND_TQ)
        k = k_ref[pl.ds(start, BAND_WIN), :]
        v = v_ref[pl.ds(start, BAND_WIN), :]
        scores = lax.dot_general(q_ref[pl.ds(t0, BAND_TQ), :], k, (((1,), (1,)), ((), ())),
                                 preferred_element_type=F32)
        scores = scores * (SCALE * LOG2E) + bias_ref[0, pl.ds(bias_row, BAND_TQ), :]
        p = jnp.exp2(scores - jnp.max(scores, axis=-1, keepdims=True))
        inv_l = 1.0 / jnp.sum(p, axis=-1, keepdims=True)
        out = jnp.dot(p.astype(BF16), v, preferred_element_type=F32) * inv_l
        o_ref[pl.ds(t0, BAND_TQ), :] = out.astype(o_ref.dtype)
        return carry

    lax.fori_loop(0, n_tiles, tile, 0, unroll=2)


def _band_attention(q, kv, bias, batch, seq):
    head_block = lambda col0: pl.BlockSpec((seq, HEAD_DIM), lambda b, h: (b, col0 + h))
    return pl.pallas_call(
        _band_attention_kernel,
        grid=(batch, N_HEADS),
        in_specs=[head_block(0), head_block(0), head_block(N_HEADS),
                  pl.BlockSpec((1, BAND_WIN, BAND_WIN), lambda b, h: (h, 0, 0))],
        out_specs=head_block(0),
        out_shape=jax.ShapeDtypeStruct((batch * seq, D_MODEL), BF16),
        compiler_params=_params("parallel", "parallel"),
        name="band_attention",
    )(q, kv, kv, bias)


def _rms_norm_kernel(x_ref, g_ref, o_ref):
    o_ref[...] = _rms_norm_f32(x_ref[...], g_ref[...])


def _rms_norm(x, g):
    m, d = x.shape
    return pl.pallas_call(
        _rms_norm_kernel,
        grid=(m // TILE_M,),
        in_specs=[pl.BlockSpec((TILE_M, d), lambda i: (i, 0)),
                  pl.BlockSpec((1, d), lambda i: (0, 0))],
        out_specs=pl.BlockSpec((TILE_M, d), lambda i: (i, 0)),
        out_shape=jax.ShapeDtypeStruct((m, d), F32),
        compiler_params=_params("parallel"),
        name="final_rms_norm",
    )(x, g.reshape(1, d))


def kernel(x, g_ffn, w_ffn_gate, w_ffn_up, w_ffn_down, g_mix, w_qkv_a, w_o_a, g_kv,
           w_kv_shared, w_q_b, w_o_b, rel_bias_b, g_final):
    batch, seq, d = x.shape
    depth = g_ffn.shape[0]
    n_a = w_qkv_a.shape[0]

    w_gate, w_up, w_down = (w.astype(BF16) for w in (w_ffn_gate, w_ffn_up, w_ffn_down))

    h = x.reshape(batch * seq, d)
    kv = None
    for layer in range(depth):
        h = _ffn_half_step(h, g_ffn[layer, 0], w_gate, w_up, w_down, (layer, 0))
        if layer < n_a:
            qkv = _norm_matmul(h, g_mix[layer], w_qkv_a, (layer,))
            mix = _stick_breaking_attention(qkv, batch, seq)
            h = _matmul_residual(mix, w_o_a, h, (layer,))
        else:
            lb = layer - n_a
            if kv is None:
                kv = _norm_matmul(h, g_kv, w_kv_shared)
            q = _norm_matmul(h, g_mix[layer], w_q_b, (lb,))
            mix = _band_attention(q, kv, _band_bias(rel_bias_b[lb]), batch, seq)
            h = _matmul_residual(mix, w_o_b, h, (lb,))
        h = _ffn_half_step(h, g_ffn[layer, 1], w_gate, w_up, w_down, (layer, 1))
    return _rms_norm(h, g_final).reshape(batch, seq, d)
```

```python
import math

import jax
import jax.numpy as jnp
from jax import lax
from jax.experimental import pallas as pl
from jax.experimental.pallas import tpu as pltpu

D_MODEL = 2048
N_HEADS = 16
HEAD_DIM = D_MODEL // N_HEADS
CHUNK = 64
LEFT_CHUNKS = 8
REL_CLIP = 256
EPS = 1e-6
SCALE = HEAD_DIM ** -0.5
LOG2E = math.log2(math.e)

BF16 = jnp.bfloat16
F32 = jnp.float32

VMEM_LIMIT_BYTES = 56 * 1024 * 1024

TILE_M = 1024
TILE_N = 1024
TILE_F = 256
FFN_DOWN_COLS = 512

SB_TQ = 256
SB_KCHUNK = 2 * SB_TQ
SB_KSUB = 128

BAND_TQ = 256
BAND_WIN = BAND_TQ + LEFT_CHUNKS * CHUNK
BAND_SHIFTS = LEFT_CHUNKS * CHUNK // BAND_TQ + 1
BIAS_ROWS = 8

assert REL_CLIP <= BAND_WIN and BAND_WIN % 128 == 0


def _params(*semantics):
    return pltpu.CompilerParams(dimension_semantics=semantics,
                                vmem_limit_bytes=VMEM_LIMIT_BYTES)


def _stacked_spec(lead, block, index_map):
    lead = tuple(lead)
    return pl.BlockSpec((None,) * len(lead) + tuple(block),
                        lambda *grid_ids: lead + tuple(index_map(*grid_ids)))


def _rms_norm_f32(x, g):
    y = x * lax.rsqrt(jnp.mean(x * x, axis=-1, keepdims=True) + EPS)
    return y * g


def _norm_matmul_kernel(x_ref, g_ref, w_ref, o_ref, xn_ref):
    @pl.when(pl.program_id(1) == 0)
    def _():
        xn_ref[...] = _rms_norm_f32(x_ref[...], g_ref[...]).astype(BF16)

    o_ref[...] = jnp.dot(xn_ref[...], w_ref[...].astype(BF16),
                         preferred_element_type=F32).astype(o_ref.dtype)


def _norm_matmul(x, g, w, lead=()):
    m, d = x.shape
    n = w.shape[-1]
    return pl.pallas_call(
        _norm_matmul_kernel,
        grid=(m // TILE_M, n // TILE_N),
        in_specs=[pl.BlockSpec((TILE_M, d), lambda i, j: (i, 0)),
                  pl.BlockSpec((1, d), lambda i, j: (0, 0)),
                  _stacked_spec(lead, (d, TILE_N), lambda i, j: (0, j))],
        out_specs=pl.BlockSpec((TILE_M, TILE_N), lambda i, j: (i, j)),
        out_shape=jax.ShapeDtypeStruct((m, n), BF16),
        scratch_shapes=[pltpu.VMEM((TILE_M, d), BF16)],
        compiler_params=_params("parallel", "arbitrary"),
        name="norm_matmul",
    )(x, g.reshape(1, d), w)


def _matmul_residual_kernel(a_ref, w_ref, r_ref, o_ref, wb_ref):
    @pl.when(pl.program_id(1) == 0)
    def _():
        wb_ref[...] = w_ref[...].astype(BF16)

    o_ref[...] = r_ref[...] + jnp.dot(a_ref[...], wb_ref[...], preferred_element_type=F32)


def _matmul_residual(a, w, res, lead=()):
    m, k = a.shape
    n = w.shape[-1]
    return pl.pallas_call(
        _matmul_residual_kernel,
        grid=(n // TILE_N, m // TILE_M),
        in_specs=[pl.BlockSpec((TILE_M, k), lambda j, i: (i, 0)),
                  _stacked_spec(lead, (k, TILE_N), lambda j, i: (0, j)),
                  pl.BlockSpec((TILE_M, TILE_N), lambda j, i: (i, j))],
        out_specs=pl.BlockSpec((TILE_M, TILE_N), lambda j, i: (i, j)),
        out_shape=jax.ShapeDtypeStruct((m, n), F32),
        scratch_shapes=[pltpu.VMEM((k, TILE_N), BF16)],
        compiler_params=_params("parallel", "arbitrary"),
        name="matmul_residual",
    )(a, w, res)


def _ffn_kernel(h_ref, g_ref, wg_ref, wu_ref, wd_ref, o_ref, xn_ref):
    j = pl.program_id(1)

    @pl.when(j == 0)
    def _():
        xn_ref[...] = _rms_norm_f32(h_ref[...], g_ref[...]).astype(BF16)
        o_ref[...] = jnp.zeros_like(o_ref)

    xn = xn_ref[...]
    gate = jnp.dot(xn, wg_ref[...].astype(BF16), preferred_element_type=F32)
    up = jnp.dot(xn, wu_ref[...].astype(BF16), preferred_element_type=F32)
    hidden = (gate * (1.0 / (1.0 + jnp.exp(-gate))) * up).astype(BF16)
    for n0 in range(0, o_ref.shape[1], FFN_DOWN_COLS):
        cols = slice(n0, n0 + FFN_DOWN_COLS)
        o_ref[:, cols] += jnp.dot(hidden, wd_ref[:, cols].astype(BF16),
                                  preferred_element_type=F32)

    @pl.when(j == pl.num_programs(1) - 1)
    def _():
        o_ref[...] = h_ref[...] + 0.5 * o_ref[...]


def _ffn_half_step(h, g, w_gate, w_up, w_down, lead):
    m, d = h.shape
    f = w_gate.shape[-1]
    return pl.pallas_call(
        _ffn_kernel,
        grid=(m // TILE_M, f // TILE_F),
        in_specs=[pl.BlockSpec((TILE_M, d), lambda i, j: (i, 0),
                               pipeline_mode=pl.Buffered(1)),
                  pl.BlockSpec((1, d), lambda i, j: (0, 0)),
                  _stacked_spec(lead, (d, TILE_F), lambda i, j: (0, j)),
                  _stacked_spec(lead, (d, TILE_F), lambda i, j: (0, j)),
                  _stacked_spec(lead, (TILE_F, d), lambda i, j: (j, 0))],
        out_specs=pl.BlockSpec((TILE_M, d), lambda i, j: (i, 0)),
        out_shape=jax.ShapeDtypeStruct((m, d), F32),
        scratch_shapes=[pltpu.VMEM((TILE_M, d), BF16)],
        compiler_params=_params("parallel", "arbitrary"),
        name="ffn_half_step",
    )(h, g.reshape(1, d), w_gate, w_up, w_down)


def _sb_key_block(q, k_ref, v_ref, start, width, mask, suffix_mat, acc_ref, csum_ref):
    ks = k_ref[pl.ds(start, width), :]
    vs = v_ref[pl.ds(start, width), :]
    w = lax.dot_general(q, ks, (((1,), (1,)), ((), ())),
                        preferred_element_type=F32) * (SCALE * LOG2E)
    sp = jnp.maximum(w, 0.0) + jnp.log2(1.0 + jnp.exp2(-jnp.abs(w)))
    sp_sum = sp if mask is None else jnp.where(mask, sp, 0.0)

    csum = csum_ref[...]
    tails = []
    for sub in reversed(range(width // SB_KSUB)):
        blk = sp_sum[:, sub * SB_KSUB:(sub + 1) * SB_KSUB]
        hi = blk.astype(BF16)
        lo = (blk - hi.astype(F32)).astype(BF16)
        sums = jnp.dot(jnp.concatenate([hi, lo], axis=1), suffix_mat,
                       preferred_element_type=F32)
        tails.append(sums[:, :SB_KSUB] + csum)
        csum = csum + sums[:, SB_KSUB:]
    csum_ref[...] = csum
    tail = jnp.concatenate(tails[::-1], axis=1)

    a = jnp.exp2((w - sp) - tail)
    if mask is not None:
        a = jnp.where(mask, a, 0.0)
    acc_ref[...] += jnp.dot(a.astype(BF16), vs, preferred_element_type=F32)


def _stick_breaking_kernel(q_ref, k_ref, v_ref, o_ref, acc_ref, csum_ref):
    seq = q_ref.shape[0]

    row = lax.broadcasted_iota(jnp.int32, (2 * SB_KSUB, 2 * SB_KSUB), 0) & (SB_KSUB - 1)
    col = lax.broadcasted_iota(jnp.int32, (2 * SB_KSUB, 2 * SB_KSUB), 1)
    suffix_mat = jnp.where((col >= SB_KSUB) | (row > col), 1.0, 0.0).astype(BF16)

    def row_pair(p, carry):
        base = pl.multiple_of(p * SB_KCHUNK, SB_KCHUNK)
        acc_ref[...] = jnp.zeros_like(acc_ref)
        csum_ref[...] = jnp.zeros_like(csum_ref)
        for row_offset in range(0, SB_KCHUNK, SB_TQ):
            rows = pl.ds(row_offset, SB_TQ)
            width = row_offset + SB_TQ
            r = lax.broadcasted_iota(jnp.int32, (SB_TQ, width), 0)
            c = lax.broadcasted_iota(jnp.int32, (SB_TQ, width), 1)
            q = q_ref[pl.ds(pl.multiple_of(base + row_offset, SB_TQ), SB_TQ), :]
            _sb_key_block(q, k_ref, v_ref, base, width, c < r + row_offset, suffix_mat,
                          acc_ref.at[rows], csum_ref.at[rows])

        q = q_ref[pl.ds(base, SB_KCHUNK), :]

        def full_chunk(n, carry):
            start = pl.multiple_of((p - 1 - n) * SB_KCHUNK, SB_KCHUNK)
            _sb_key_block(q, k_ref, v_ref, start, SB_KCHUNK, None, suffix_mat, acc_ref, csum_ref)
            return carry

        lax.fori_loop(0, p, full_chunk, 0)
        o_ref[pl.ds(base, SB_KCHUNK), :] = acc_ref[...].astype(o_ref.dtype)
        return carry

    lax.fori_loop(0, seq // SB_KCHUNK, row_pair, 0)


def _stick_breaking_attention(qkv, batch, seq):
    head_block = lambda col0: pl.BlockSpec((seq, HEAD_DIM), lambda b, h: (b, col0 + h))
    return pl.pallas_call(
        _stick_breaking_kernel,
        grid=(batch, N_HEADS),
        in_specs=[head_block(0), head_block(N_HEADS), head_block(2 * N_HEADS)],
        out_specs=head_block(0),
        out_shape=jax.ShapeDtypeStruct((batch * seq, D_MODEL), BF16),
        scratch_shapes=[pltpu.VMEM((SB_KCHUNK, HEAD_DIM), F32),
                        pltpu.VMEM((SB_KCHUNK, SB_KSUB), F32)],
        compiler_params=_params("parallel", "parallel"),
        name="stick_breaking_attention",
    )(qkv, qkv, qkv)


def _band_bias_kernel(y_ref, o_ref):
    y = jnp.broadcast_to(y_ref[0], (BIAS_ROWS, 2 * BAND_WIN))
    c_chunk = lax.shift_right_logical(
        lax.broadcasted_iota(jnp.int32, (BIAS_ROWS, BAND_WIN), 1), CHUNK.bit_length() - 1)

    def rows(g, carry):
        u0 = pl.multiple_of(g * BIAS_ROWS, BIAS_ROWS)
        rolled = pltpu.roll(y, u0, 1, stride=1, stride_axis=0)
        chunk_gap = lax.shift_right_logical(u0, CHUNK.bit_length() - 1) - c_chunk
        valid = (chunk_gap >= 0) & (chunk_gap <= LEFT_CHUNKS)
        o_ref[0, pl.ds(u0, BIAS_ROWS), :] = jnp.where(valid, rolled[:, BAND_WIN:] * LOG2E,
                                                      -jnp.inf)
        return carry

    lax.fori_loop(0, BAND_WIN // BIAS_ROWS, rows, 0, unroll=8)


def _band_bias(rel_bias):
    edge = BAND_WIN - REL_CLIP
    y = jnp.concatenate(
        [jnp.broadcast_to(rel_bias[:, -1:], (N_HEADS, edge)), rel_bias[:, ::-1],
         jnp.broadcast_to(rel_bias[:, :1], (N_HEADS, edge - 1))], axis=1)
    return pl.pallas_call(
        _band_bias_kernel,
        grid=(N_HEADS,),
        in_specs=[pl.BlockSpec((1, 1, 2 * BAND_WIN), lambda h: (h, 0, 0))],
        out_specs=pl.BlockSpec((1, BAND_WIN, BAND_WIN), lambda h: (h, 0, 0)),
        out_shape=jax.ShapeDtypeStruct((N_HEADS, BAND_WIN, BAND_WIN), F32),
        compiler_params=_params("parallel"),
        name="band_bias",
    )(y.reshape(N_HEADS, 1, 2 * BAND_WIN))


def _band_attention_kernel(q_ref, k_ref, v_ref, bias_ref, o_ref):
    n_tiles = q_ref.shape[0] // BAND_TQ

    def tile(i, carry):
        t0 = pl.multiple_of(i * BAND_TQ, BAND_TQ)
        start = pl.multiple_of(jnp.maximum(t0 - LEFT_CHUNKS * CHUNK, 0), BAND_TQ)
        bias_row = pl.multiple_of(t0 - start, BAND_TQ)
        k = k_ref[pl.ds(start, BAND_WIN), :]
        v = v_ref[pl.ds(start, BAND_WIN), :]
        scores = lax.dot_general(q_ref[pl.ds(t0, BAND_TQ), :], k, (((1,), (1,)), ((), ())),
                                 preferred_element_type=F32)
        scores = scores * (SCALE * LOG2E) + bias_ref[0, pl.ds(bias_row, BAND_TQ), :]
        p = jnp.exp2(scores - jnp.max(scores, axis=-1, keepdims=True))
        inv_l = 1.0 / jnp.sum(p, axis=-1, keepdims=True)
        out = jnp.dot(p.astype(BF16), v, preferred_element_type=F32) * inv_l
        o_ref[pl.ds(t0, BAND_TQ), :] = out.astype(o_ref.dtype)
        return carry

    lax.fori_loop(0, n_tiles, tile, 0, unroll=4)


def _band_attention(q, kv, bias, batch, seq):
    head_block = lambda col0: pl.BlockSpec((seq, HEAD_DIM), lambda b, h: (b, col0 + h))
    return pl.pallas_call(
        _band_attention_kernel,
        grid=(batch, N_HEADS),
        in_specs=[head_block(0), head_block(0), head_block(N_HEADS),
                  pl.BlockSpec((1, BAND_WIN, BAND_WIN), lambda b, h: (h, 0, 0))],
        out_specs=head_block(0),
        out_shape=jax.ShapeDtypeStruct((batch * seq, D_MODEL), BF16),
        compiler_params=_params("parallel", "parallel"),
        name="band_attention",
    )(q, kv, kv, bias)


def _rms_norm_kernel(x_ref, g_ref, o_ref):
    o_ref[...] = _rms_norm_f32(x_ref[...], g_ref[...])


def _rms_norm(x, g):
    m, d = x.shape
    return pl.pallas_call(
        _rms_norm_kernel,
        grid=(m // TILE_M,),
        in_specs=[pl.BlockSpec((TILE_M, d), lambda i: (i, 0)),
                  pl.BlockSpec((1, d), lambda i: (0, 0))],
        out_specs=pl.BlockSpec((TILE_M, d), lambda i: (i, 0)),
        out_shape=jax.ShapeDtypeStruct((m, d), F32),
        compiler_params=_params("parallel"),
        name="final_rms_norm",
    )(x, g.reshape(1, d))


def kernel(x, g_ffn, w_ffn_gate, w_ffn_up, w_ffn_down, g_mix, w_qkv_a, w_o_a, g_kv,
           w_kv_shared, w_q_b, w_o_b, rel_bias_b, g_final):
    batch, seq, d = x.shape
    depth = g_ffn.shape[0]
    n_a = w_qkv_a.shape[0]

    w_gate, w_up, w_down = w_ffn_gate, w_ffn_up, w_ffn_down

    h = x.reshape(batch * seq, d)
    kv = None
    for layer in range(depth):
        h = _ffn_half_step(h, g_ffn[layer, 0], w_gate, w_up, w_down, (layer, 0))
        if layer < n_a:
            qkv = _norm_matmul(h, g_mix[layer], w_qkv_a, (layer,))
            mix = _stick_breaking_attention(qkv, batch, seq)
            h = _matmul_residual(mix, w_o_a, h, (layer,))
        else:
            lb = layer - n_a
            if kv is None:
                kv = _norm_matmul(h, g_kv, w_kv_shared)
            q = _norm_matmul(h, g_mix[layer], w_q_b, (lb,))
            mix = _band_attention(q, kv, _band_bias(rel_bias_b[lb]), batch, seq)
            h = _matmul_residual(mix, w_o_b, h, (lb,))
        h = _ffn_half_step(h, g_ffn[layer, 1], w_gate, w_up, w_down, (layer, 1))
    return _rms_norm(h, g_final).reshape(batch, seq, d)
```

```python
import functools
import math

import jax
import jax.numpy as jnp
from jax import lax
from jax.experimental import pallas as pl
from jax.experimental.pallas import tpu as pltpu

D_MODEL = 2048
N_HEADS = 16
HEAD_DIM = D_MODEL // N_HEADS
CHUNK = 64
LEFT_CHUNKS = 8
REL_CLIP = 256
EPS = 1e-6
LOG2E = math.log2(math.e)
QUERY_SCALE = HEAD_DIM ** -0.5 * LOG2E

BF16 = jnp.bfloat16
F32 = jnp.float32

VMEM_LIMIT_BYTES = 56 * 1024 * 1024

TILE_M = 1024
TILE_N = 1024
TILE_F = 256
FFN_DOWN_COLS = 512

SB_TQ = 256
SB_KCHUNK = 2 * SB_TQ
SB_KSTEP = 256
SB_KSUB = 128
SB_DEAD_LOG2 = 160.0

BAND_TQ = 256
BAND_WIN = BAND_TQ + LEFT_CHUNKS * CHUNK
BAND_SHIFTS = LEFT_CHUNKS * CHUNK // BAND_TQ + 1
BIAS_ROWS = 8

assert REL_CLIP <= BAND_WIN and BAND_WIN % 128 == 0


def _params(*semantics):
    return pltpu.CompilerParams(dimension_semantics=semantics,
                                vmem_limit_bytes=VMEM_LIMIT_BYTES)


def _stacked_spec(lead, block, index_map):
    lead = tuple(lead)
    return pl.BlockSpec((None,) * len(lead) + tuple(block),
                        lambda *grid_ids: lead + tuple(index_map(*grid_ids)))


def _rms_norm_f32(x, g):
    y = x * lax.rsqrt(jnp.mean(x * x, axis=-1, keepdims=True) + EPS)
    return y * g


def _norm_matmul_kernel(x_ref, g_ref, w_ref, o_ref, xn_ref, *, query_tiles):
    @pl.when(pl.program_id(1) == 0)
    def _():
        xn_ref[...] = _rms_norm_f32(x_ref[...], g_ref[...]).astype(BF16)

    acc = jnp.dot(xn_ref[...], w_ref[...].astype(BF16), preferred_element_type=F32)
    if query_tiles:
        acc = acc * jnp.where(pl.program_id(1) < query_tiles, QUERY_SCALE, 1.0)
    o_ref[...] = acc.astype(o_ref.dtype)


def _norm_matmul(x, g, w, lead=(), query_cols=0):
    m, d = x.shape
    n = w.shape[-1]
    return pl.pallas_call(
        functools.partial(_norm_matmul_kernel, query_tiles=query_cols // TILE_N),
        grid=(m // TILE_M, n // TILE_N),
        in_specs=[pl.BlockSpec((TILE_M, d), lambda i, j: (i, 0)),
                  pl.BlockSpec((1, d), lambda i, j: (0, 0)),
                  _stacked_spec(lead, (d, TILE_N), lambda i, j: (0, j))],
        out_specs=pl.BlockSpec((TILE_M, TILE_N), lambda i, j: (i, j)),
        out_shape=jax.ShapeDtypeStruct((m, n), BF16),
        scratch_shapes=[pltpu.VMEM((TILE_M, d), BF16)],
        compiler_params=_params("parallel", "arbitrary"),
        name="norm_matmul",
    )(x, g.reshape(1, d), w)


def _matmul_residual_kernel(a_ref, w_ref, r_ref, o_ref, wb_ref):
    @pl.when(pl.program_id(1) == 0)
    def _():
        wb_ref[...] = w_ref[...].astype(BF16)

    o_ref[...] = r_ref[...] + jnp.dot(a_ref[...], wb_ref[...], preferred_element_type=F32)


def _matmul_residual(a, w, res, lead=()):
    m, k = a.shape
    n = w.shape[-1]
    return pl.pallas_call(
        _matmul_residual_kernel,
        grid=(n // TILE_N, m // TILE_M),
        in_specs=[pl.BlockSpec((TILE_M, k), lambda j, i: (i, 0)),
                  _stacked_spec(lead, (k, TILE_N), lambda j, i: (0, j)),
                  pl.BlockSpec((TILE_M, TILE_N), lambda j, i: (i, j))],
        out_specs=pl.BlockSpec((TILE_M, TILE_N), lambda j, i: (i, j)),
        out_shape=jax.ShapeDtypeStruct((m, n), F32),
        scratch_shapes=[pltpu.VMEM((k, TILE_N), BF16)],
        compiler_params=_params("parallel", "arbitrary"),
        name="matmul_residual",
    )(a, w, res)


def _ffn_kernel(h_ref, g_ref, wg_ref, wu_ref, wd_ref, *rest, out_norm):
    g_out_ref, o_ref, xn_ref = rest if out_norm else (None,) + rest
    j = pl.program_id(1)

    @pl.when(j == 0)
    def _():
        xn_ref[...] = _rms_norm_f32(h_ref[...], g_ref[...]).astype(BF16)
        o_ref[...] = jnp.zeros_like(o_ref)

    xn = xn_ref[...]
    gate = jnp.dot(xn, wg_ref[...].astype(BF16), preferred_element_type=F32)
    up = jnp.dot(xn, wu_ref[...].astype(BF16), preferred_element_type=F32)
    hidden = (gate * (1.0 / (1.0 + jnp.exp(-gate))) * up).astype(BF16)
    for n0 in range(0, o_ref.shape[1], FFN_DOWN_COLS):
        cols = slice(n0, n0 + FFN_DOWN_COLS)
        o_ref[:, cols] += jnp.dot(hidden, wd_ref[:, cols].astype(BF16),
                                  preferred_element_type=F32)

    @pl.when(j == pl.num_programs(1) - 1)
    def _():
        y = h_ref[...] + 0.5 * o_ref[...]
        o_ref[...] = _rms_norm_f32(y, g_out_ref[...]) if out_norm else y


def _ffn_half_step(h, g, w_gate, w_up, w_down, lead, g_out=None):
    m, d = h.shape
    f = w_gate.shape[-1]
    gain_spec = pl.BlockSpec((1, d), lambda i, j: (0, 0))
    out_norm = g_out is not None
    return pl.pallas_call(
        functools.partial(_ffn_kernel, out_norm=out_norm),
        grid=(m // TILE_M, f // TILE_F),
        in_specs=[pl.BlockSpec((TILE_M, d), lambda i, j: (i, 0),
                               pipeline_mode=pl.Buffered(1)),
                  gain_spec,
                  _stacked_spec(lead, (d, TILE_F), lambda i, j: (0, j)),
                  _stacked_spec(lead, (d, TILE_F), lambda i, j: (0, j)),
                  _stacked_spec(lead, (TILE_F, d), lambda i, j: (j, 0))]
        + [gain_spec] * out_norm,
        out_specs=pl.BlockSpec((TILE_M, d), lambda i, j: (i, 0)),
        out_shape=jax.ShapeDtypeStruct((m, d), F32),
        scratch_shapes=[pltpu.VMEM((TILE_M, d), BF16)],
        compiler_params=_params("parallel", "arbitrary"),
        name="ffn_half_step",
    )(h, g.reshape(1, d), w_gate, w_up, w_down, *([g_out.reshape(1, d)] if out_norm else []))


def _sb_key_block(q, k_ref, v_ref, start, width, mask, suffix_mat, acc_ref, csum_ref):
    ks = k_ref[pl.ds(start, width), :]
    vs = v_ref[pl.ds(start, width), :]
    w = lax.dot_general(q, ks, (((1,), (1,)), ((), ())), preferred_element_type=F32)
    sp = jnp.maximum(w, 0.0) + jnp.log2(1.0 + jnp.exp2(-jnp.abs(w)))
    sp_sum = sp if mask is None else jnp.where(mask, sp, 0.0)

    csum = csum_ref[...]
    tails = []
    for sub in reversed(range(width // SB_KSUB)):
        blk = sp_sum[:, sub * SB_KSUB:(sub + 1) * SB_KSUB]
        hi = blk.astype(BF16)
        lo = (blk - hi.astype(F32)).astype(BF16)
        sums = jnp.dot(jnp.concatenate([hi, lo], axis=1), suffix_mat,
                       preferred_element_type=F32)
        tails.append(sums[:, :SB_KSUB] + csum)
        csum = csum + sums[:, SB_KSUB:]
    csum_ref[...] = csum
    tail = jnp.concatenate(tails[::-1], axis=1)

    a = jnp.exp2((w - sp) - tail)
    if mask is not None:
        a = jnp.where(mask, a, 0.0)
    acc_ref[...] += jnp.dot(a.astype(BF16), vs, preferred_element_type=F32)


def _stick_breaking_kernel(q_ref, k_ref, v_ref, o_ref, acc_ref, csum_ref):
    seq = q_ref.shape[0]

    row = lax.broadcasted_iota(jnp.int32, (2 * SB_KSUB, 2 * SB_KSUB), 0) & (SB_KSUB - 1)
    col = lax.broadcasted_iota(jnp.int32, (2 * SB_KSUB, 2 * SB_KSUB), 1)
    suffix_mat = jnp.where((col >= SB_KSUB) | (row > col), 1.0, 0.0).astype(BF16)

    def row_pair(p, carry):
        base = pl.multiple_of(p * SB_KCHUNK, SB_KCHUNK)
        acc_ref[...] = jnp.zeros_like(acc_ref)
        csum_ref[...] = jnp.zeros_like(csum_ref)
        for row_offset in range(0, SB_KCHUNK, SB_TQ):
            rows = pl.ds(row_offset, SB_TQ)
            width = row_offset + SB_TQ
            r = lax.broadcasted_iota(jnp.int32, (SB_TQ, width), 0)
            c = lax.broadcasted_iota(jnp.int32, (SB_TQ, width), 1)
            q = q_ref[pl.ds(pl.multiple_of(base + row_offset, SB_TQ), SB_TQ), :]
            _sb_key_block(q, k_ref, v_ref, base, width, c < r + row_offset, suffix_mat,
                          acc_ref.at[rows], csum_ref.at[rows])

        q = q_ref[pl.ds(base, SB_KCHUNK), :]
        n_steps = base // SB_KSTEP

        def keys_left(state):
            n, live = state
            return jnp.logical_and(n < n_steps, live)

        def key_step(state):
            n, _ = state
            start = pl.multiple_of(base - (n + 1) * SB_KSTEP, SB_KSTEP)
            _sb_key_block(q, k_ref, v_ref, start, SB_KSTEP, None, suffix_mat, acc_ref, csum_ref)
            return n + 1, jnp.min(csum_ref[...]) < SB_DEAD_LOG2

        lax.while_loop(keys_left, key_step, (jnp.int32(0), True))
        o_ref[pl.ds(base, SB_KCHUNK), :] = acc_ref[...].astype(o_ref.dtype)
        return carry

    lax.fori_loop(0, seq // SB_KCHUNK, row_pair, 0)


def _stick_breaking_attention(qkv, batch, seq):
    head_block = lambda col0: pl.BlockSpec((seq, HEAD_DIM), lambda b, h: (b, col0 + h))
    return pl.pallas_call(
        _stick_breaking_kernel,
        grid=(batch, N_HEADS),
        in_specs=[head_block(0), head_block(N_HEADS), head_block(2 * N_HEADS)],
        out_specs=head_block(0),
        out_shape=jax.ShapeDtypeStruct((batch * seq, D_MODEL), BF16),
        scratch_shapes=[pltpu.VMEM((SB_KCHUNK, HEAD_DIM), F32),
                        pltpu.VMEM((SB_KCHUNK, SB_KSUB), F32)],
        compiler_params=_params("parallel", "parallel"),
        name="stick_breaking_attention",
    )(qkv, qkv, qkv)


def _band_bias_kernel(y_ref, o_ref):
    y = jnp.broadcast_to(y_ref[0], (BIAS_ROWS, 2 * BAND_WIN))
    c_chunk = lax.shift_right_logical(
        lax.broadcasted_iota(jnp.int32, (BIAS_ROWS, BAND_WIN), 1), CHUNK.bit_length() - 1)

    def rows(g, carry):
        u0 = pl.multiple_of(g * BIAS_ROWS, BIAS_ROWS)
        rolled = pltpu.roll(y, u0, 1, stride=1, stride_axis=0)
        chunk_gap = lax.shift_right_logical(u0, CHUNK.bit_length() - 1) - c_chunk
        valid = (chunk_gap >= 0) & (chunk_gap <= LEFT_CHUNKS)
        o_ref[0, pl.ds(u0, BIAS_ROWS), :] = jnp.where(valid, rolled[:, BAND_WIN:] * LOG2E,
                                                      -jnp.inf)
        return carry

    lax.fori_loop(0, BAND_WIN // BIAS_ROWS, rows, 0, unroll=8)


def _band_bias(rel_bias):
    edge = BAND_WIN - REL_CLIP
    y = jnp.concatenate(
        [jnp.broadcast_to(rel_bias[:, -1:], (N_HEADS, edge)), rel_bias[:, ::-1],
         jnp.broadcast_to(rel_bias[:, :1], (N_HEADS, edge - 1))], axis=1)
    return pl.pallas_call(
        _band_bias_kernel,
        grid=(N_HEADS,),
        in_specs=[pl.BlockSpec((1, 1, 2 * BAND_WIN), lambda h: (h, 0, 0))],
        out_specs=pl.BlockSpec((1, BAND_WIN, BAND_WIN), lambda h: (h, 0, 0)),
        out_shape=jax.ShapeDtypeStruct((N_HEADS, BAND_WIN, BAND_WIN), F32),
        compiler_params=_params("parallel"),
        name="band_bias",
    )(y.reshape(N_HEADS, 1, 2 * BAND_WIN))


def _band_attention_kernel(q_ref, k_ref, v_ref, bias_ref, o_ref):
    n_tiles = q_ref.shape[0] // BAND_TQ

    def tile(i, carry):
        t0 = pl.multiple_of(i * BAND_TQ, BAND_TQ)
        start = pl.multiple_of(jnp.maximum(t0 - LEFT_CHUNKS * CHUNK, 0), BAND_TQ)
        bias_row = pl.multiple_of(t0 - start, BAND_TQ)
        k = k_ref[pl.ds(start, BAND_WIN), :]
        v = v_ref[pl.ds(start, BAND_WIN), :]
        scores = lax.dot_general(q_ref[pl.ds(t0, BAND_TQ), :], k, (((1,), (1,)), ((), ())),
                                 preferred_element_type=F32)
        scores = scores + bias_ref[0, pl.ds(bias_row, BAND_TQ), :]
        p = jnp.exp2(scores - jnp.max(scores, axis=-1, keepdims=True))
        inv_l = 1.0 / jnp.sum(p, axis=-1, keepdims=True)
        out = jnp.dot(p.astype(BF16), v, preferred_element_type=F32) * inv_l
        o_ref[pl.ds(t0, BAND_TQ), :] = out.astype(o_ref.dtype)
        return carry

    lax.fori_loop(0, n_tiles, tile, 0, unroll=4)


def _band_attention(q, kv, bias, batch, seq):
    head_block = lambda col0: pl.BlockSpec((seq, HEAD_DIM), lambda b, h: (b, col0 + h))
    return pl.pallas_call(
        _band_attention_kernel,
        grid=(batch, N_HEADS),
        in_specs=[head_block(0), head_block(0), head_block(N_HEADS),
                  pl.BlockSpec((1, BAND_WIN, BAND_WIN), lambda b, h: (h, 0, 0))],
        out_specs=head_block(0),
        out_shape=jax.ShapeDtypeStruct((batch * seq, D_MODEL), BF16),
        compiler_params=_params("parallel", "parallel"),
        name="band_attention",
    )(q, kv, kv, bias)


def kernel(x, g_ffn, w_ffn_gate, w_ffn_up, w_ffn_down, g_mix, w_qkv_a, w_o_a, g_kv,
           w_kv_shared, w_q_b, w_o_b, rel_bias_b, g_final):
    batch, seq, d = x.shape
    depth = g_ffn.shape[0]
    n_a = w_qkv_a.shape[0]

    w_gate, w_up, w_down = w_ffn_gate, w_ffn_up, w_ffn_down

    h = x.reshape(batch * seq, d)
    kv = None
    for layer in range(depth):
        h = _ffn_half_step(h, g_ffn[layer, 0], w_gate, w_up, w_down, (layer, 0))
        if layer < n_a:
            qkv = _norm_matmul(h, g_mix[layer], w_qkv_a, (layer,), query_cols=d)
            mix = _stick_breaking_attention(qkv, batch, seq)
            h = _matmul_residual(mix, w_o_a, h, (layer,))
        else:
            lb = layer - n_a
            if kv is None:
                kv = _norm_matmul(h, g_kv, w_kv_shared)
            q = _norm_matmul(h, g_mix[layer], w_q_b, (lb,), query_cols=d)
            mix = _band_attention(q, kv, _band_bias(rel_bias_b[lb]), batch, seq)
            h = _matmul_residual(mix, w_o_b, h, (lb,))
        h = _ffn_half_step(h, g_ffn[layer, 1], w_gate, w_up, w_down, (layer, 1),
                           g_out=g_final if layer == depth - 1 else None)
    return h.reshape(batch, seq, d)
```

```python
import functools
import math

import jax
import jax.numpy as jnp
from jax import lax
from jax.experimental import pallas as pl
from jax.experimental.pallas import tpu as pltpu

D_MODEL = 2048
N_HEADS = 16
HEAD_DIM = D_MODEL // N_HEADS
CHUNK = 64
LEFT_CHUNKS = 8
REL_CLIP = 256
EPS = 1e-6
LOG2E = math.log2(math.e)
QUERY_SCALE = HEAD_DIM ** -0.5 * LOG2E

BF16 = jnp.bfloat16
F32 = jnp.float32

VMEM_LIMIT_BYTES = 56 * 1024 * 1024

TILE_M = 1024
TILE_N = 1024
TILE_F = 256
FFN_DOWN_COLS = 512

SB_TQ = 256
SB_KCHUNK = 2 * SB_TQ
SB_KSTEP = 256
SB_KSUB = 128
SB_DEAD_LOG2 = 160.0

BAND_TQ = 256
BAND_WIN = BAND_TQ + LEFT_CHUNKS * CHUNK
BAND_SHIFTS = LEFT_CHUNKS * CHUNK // BAND_TQ + 1
BIAS_ROWS = 8

assert REL_CLIP <= BAND_WIN and BAND_WIN % 128 == 0


def _params(*semantics):
    return pltpu.CompilerParams(dimension_semantics=semantics,
                                vmem_limit_bytes=VMEM_LIMIT_BYTES)


def _stacked_spec(lead, block, index_map):
    lead = tuple(lead)
    return pl.BlockSpec((None,) * len(lead) + tuple(block),
                        lambda *grid_ids: lead + tuple(index_map(*grid_ids)))


def _rms_norm_f32(x, g):
    y = x * lax.rsqrt(jnp.mean(x * x, axis=-1, keepdims=True) + EPS)
    return y * g


def _norm_matmul_kernel(x_ref, g_ref, w_ref, o_ref, xn_ref, *, query_tiles):
    @pl.when(pl.program_id(1) == 0)
    def _():
        xn_ref[...] = _rms_norm_f32(x_ref[...], g_ref[...]).astype(BF16)

    acc = jnp.dot(xn_ref[...], w_ref[...].astype(BF16), preferred_element_type=F32)
    if query_tiles:
        acc = acc * jnp.where(pl.program_id(1) < query_tiles, QUERY_SCALE, 1.0)
    o_ref[...] = acc.astype(o_ref.dtype)


def _norm_matmul(x, g, w, lead=(), query_cols=0):
    m, d = x.shape
    n = w.shape[-1]
    return pl.pallas_call(
        functools.partial(_norm_matmul_kernel, query_tiles=query_cols // TILE_N),
        grid=(m // TILE_M, n // TILE_N),
        in_specs=[pl.BlockSpec((TILE_M, d), lambda i, j: (i, 0)),
                  pl.BlockSpec((1, d), lambda i, j: (0, 0)),
                  _stacked_spec(lead, (d, TILE_N), lambda i, j: (0, j))],
        out_specs=pl.BlockSpec((TILE_M, TILE_N), lambda i, j: (i, j)),
        out_shape=jax.ShapeDtypeStruct((m, n), BF16),
        scratch_shapes=[pltpu.VMEM((TILE_M, d), BF16)],
        compiler_params=_params("parallel", "arbitrary"),
        name="norm_matmul",
    )(x, g.reshape(1, d), w)


def _matmul_residual_kernel(a_ref, w_ref, r_ref, o_ref, wb_ref):
    @pl.when(pl.program_id(1) == 0)
    def _():
        wb_ref[...] = w_ref[...].astype(BF16)

    o_ref[...] = r_ref[...] + jnp.dot(a_ref[...], wb_ref[...], preferred_element_type=F32)


def _matmul_residual(a, w, res, lead=()):
    m, k = a.shape
    n = w.shape[-1]
    return pl.pallas_call(
        _matmul_residual_kernel,
        grid=(n // TILE_N, m // TILE_M),
        in_specs=[pl.BlockSpec((TILE_M, k), lambda j, i: (i, 0)),
                  _stacked_spec(lead, (k, TILE_N), lambda j, i: (0, j)),
                  pl.BlockSpec((TILE_M, TILE_N), lambda j, i: (i, j))],
        out_specs=pl.BlockSpec((TILE_M, TILE_N), lambda j, i: (i, j)),
        out_shape=jax.ShapeDtypeStruct((m, n), F32),
        scratch_shapes=[pltpu.VMEM((k, TILE_N), BF16)],
        compiler_params=_params("parallel", "arbitrary"),
        name="matmul_residual",
    )(a, w, res)


def _ffn_kernel(h_ref, g_ref, wg_ref, wu_ref, wd_ref, *rest, out_norm):
    g_out_ref, o_ref, xn_ref = rest if out_norm else (None,) + rest
    j = pl.program_id(1)

    @pl.when(j == 0)
    def _():
        xn_ref[...] = _rms_norm_f32(h_ref[...], g_ref[...]).astype(BF16)
        o_ref[...] = jnp.zeros_like(o_ref)

    xn = xn_ref[...]
    gate = jnp.dot(xn, wg_ref[...].astype(BF16), preferred_element_type=F32)
    up = jnp.dot(xn, wu_ref[...].astype(BF16), preferred_element_type=F32)
    hidden = (gate * (1.0 / (1.0 + jnp.exp(-gate))) * up).astype(BF16)
    for n0 in range(0, o_ref.shape[1], FFN_DOWN_COLS):
        cols = slice(n0, n0 + FFN_DOWN_COLS)
        o_ref[:, cols] += jnp.dot(hidden, wd_ref[:, cols].astype(BF16),
                                  preferred_element_type=F32)

    @pl.when(j == pl.num_programs(1) - 1)
    def _():
        y = h_ref[...] + 0.5 * o_ref[...]
        o_ref[...] = _rms_norm_f32(y, g_out_ref[...]) if out_norm else y


def _ffn_half_step(h, g, w_gate, w_up, w_down, lead, g_out=None):
    m, d = h.shape
    f = w_gate.shape[-1]
    gain_spec = pl.BlockSpec((1, d), lambda i, j: (0, 0))
    out_norm = g_out is not None
    return pl.pallas_call(
        functools.partial(_ffn_kernel, out_norm=out_norm),
        grid=(m // TILE_M, f // TILE_F),
        in_specs=[pl.BlockSpec((TILE_M, d), lambda i, j: (i, 0),
                               pipeline_mode=pl.Buffered(1)),
                  gain_spec,
                  _stacked_spec(lead, (d, TILE_F), lambda i, j: (0, j)),
                  _stacked_spec(lead, (d, TILE_F), lambda i, j: (0, j)),
                  _stacked_spec(lead, (TILE_F, d), lambda i, j: (j, 0))]
        + [gain_spec] * out_norm,
        out_specs=pl.BlockSpec((TILE_M, d), lambda i, j: (i, 0)),
        out_shape=jax.ShapeDtypeStruct((m, d), F32),
        scratch_shapes=[pltpu.VMEM((TILE_M, d), BF16)],
        compiler_params=_params("parallel", "arbitrary"),
        name="ffn_half_step",
    )(h, g.reshape(1, d), w_gate, w_up, w_down, *([g_out.reshape(1, d)] if out_norm else []))


def _sb_key_block(q, k_ref, v_ref, start, width, mask, suffix_mat, acc_ref, csum_ref):
    ks = k_ref[pl.ds(start, width), :]
    vs = v_ref[pl.ds(start, width), :]
    w = lax.dot_general(q, ks, (((1,), (1,)), ((), ())), preferred_element_type=F32)
    yield
    sp = jnp.maximum(w, 0.0) + jnp.log2(1.0 + jnp.exp2(-jnp.abs(w)))
    sp_sum = sp if mask is None else jnp.where(mask, sp, 0.0)
    yield

    csum = csum_ref[...]
    tails = []
    for sub in reversed(range(width // SB_KSUB)):
        blk = sp_sum[:, sub * SB_KSUB:(sub + 1) * SB_KSUB]
        hi = blk.astype(BF16)
        lo = (blk - hi.astype(F32)).astype(BF16)
        sums = jnp.dot(jnp.concatenate([hi, lo], axis=1), suffix_mat,
                       preferred_element_type=F32)
        tails.append(sums[:, :SB_KSUB] + csum)
        csum = csum + sums[:, SB_KSUB:]
    csum_ref[...] = csum
    tail = jnp.concatenate(tails[::-1], axis=1)
    yield

    a = jnp.exp2((w - sp) - tail)
    if mask is not None:
        a = jnp.where(mask, a, 0.0)
    yield
    acc_ref[...] += jnp.dot(a.astype(BF16), vs, preferred_element_type=F32)


def _interleave(*streams):
    streams = list(streams)
    while streams:
        for stream in list(streams):
            if next(stream, StopIteration) is StopIteration:
                streams.remove(stream)


def _stick_breaking_kernel(q_ref, k_ref, v_ref, o_ref, acc_ref, csum_ref):
    seq = q_ref.shape[0]

    row = lax.broadcasted_iota(jnp.int32, (2 * SB_KSUB, 2 * SB_KSUB), 0) & (SB_KSUB - 1)
    col = lax.broadcasted_iota(jnp.int32, (2 * SB_KSUB, 2 * SB_KSUB), 1)
    suffix_mat = jnp.where((col >= SB_KSUB) | (row > col), 1.0, 0.0).astype(BF16)

    def row_pair(p, carry):
        base = pl.multiple_of(p * SB_KCHUNK, SB_KCHUNK)
        acc_ref[...] = jnp.zeros_like(acc_ref)
        csum_ref[...] = jnp.zeros_like(csum_ref)
        row_tiles = [pl.ds(row_offset, SB_TQ) for row_offset in range(0, SB_KCHUNK, SB_TQ)]
        qs = [q_ref[pl.ds(pl.multiple_of(base + rows.start, SB_TQ), SB_TQ), :]
              for rows in row_tiles]

        def diagonal(q, rows):
            width = rows.start + SB_TQ
            r = lax.broadcasted_iota(jnp.int32, (SB_TQ, width), 0)
            c = lax.broadcasted_iota(jnp.int32, (SB_TQ, width), 1)
            return _sb_key_block(q, k_ref, v_ref, base, width, c < r + rows.start, suffix_mat,
                                 acc_ref.at[rows], csum_ref.at[rows])

        _interleave(*(diagonal(q, rows) for q, rows in zip(qs, row_tiles)))
        n_steps = base // SB_KSTEP

        def keys_left(state):
            n, live = state
            return jnp.logical_and(n < n_steps, live)

        def key_step(state):
            n, _ = state
            start = pl.multiple_of(base - (n + 1) * SB_KSTEP, SB_KSTEP)
            _interleave(*(_sb_key_block(q, k_ref, v_ref, start, SB_KSTEP, None, suffix_mat,
                                        acc_ref.at[rows], csum_ref.at[rows])
                          for q, rows in zip(qs, row_tiles)))
            return n + 1, jnp.min(csum_ref[...]) < SB_DEAD_LOG2

        lax.while_loop(keys_left, key_step, (jnp.int32(0), True))
        o_ref[pl.ds(base, SB_KCHUNK), :] = acc_ref[...].astype(o_ref.dtype)
        return carry

    lax.fori_loop(0, seq // SB_KCHUNK, row_pair, 0)


def _stick_breaking_attention(qkv, batch, seq):
    head_block = lambda col0: pl.BlockSpec((seq, HEAD_DIM), lambda b, h: (b, col0 + h))
    return pl.pallas_call(
        _stick_breaking_kernel,
        grid=(batch, N_HEADS),
        in_specs=[head_block(0), head_block(N_HEADS), head_block(2 * N_HEADS)],
        out_specs=head_block(0),
        out_shape=jax.ShapeDtypeStruct((batch * seq, D_MODEL), BF16),
        scratch_shapes=[pltpu.VMEM((SB_KCHUNK, HEAD_DIM), F32),
                        pltpu.VMEM((SB_KCHUNK, SB_KSUB), F32)],
        compiler_params=_params("parallel", "parallel"),
        name="stick_breaking_attention",
    )(qkv, qkv, qkv)


def _band_bias_kernel(y_ref, o_ref):
    y = jnp.broadcast_to(y_ref[0], (BIAS_ROWS, 2 * BAND_WIN))
    c_chunk = lax.shift_right_logical(
        lax.broadcasted_iota(jnp.int32, (BIAS_ROWS, BAND_WIN), 1), CHUNK.bit_length() - 1)

    def rows(g, carry):
        u0 = pl.multiple_of(g * BIAS_ROWS, BIAS_ROWS)
        rolled = pltpu.roll(y, u0, 1, stride=1, stride_axis=0)
        chunk_gap = lax.shift_right_logical(u0, CHUNK.bit_length() - 1) - c_chunk
        valid = (chunk_gap >= 0) & (chunk_gap <= LEFT_CHUNKS)
        o_ref[0, pl.ds(u0, BIAS_ROWS), :] = jnp.where(valid, rolled[:, BAND_WIN:] * LOG2E,
                                                      -jnp.inf)
        return carry

    lax.fori_loop(0, BAND_WIN // BIAS_ROWS, rows, 0, unroll=8)


def _band_bias(rel_bias):
    edge = BAND_WIN - REL_CLIP
    y = jnp.concatenate(
        [jnp.broadcast_to(rel_bias[:, -1:], (N_HEADS, edge)), rel_bias[:, ::-1],
         jnp.broadcast_to(rel_bias[:, :1], (N_HEADS, edge - 1))], axis=1)
    return pl.pallas_call(
        _band_bias_kernel,
        grid=(N_HEADS,),
        in_specs=[pl.BlockSpec((1, 1, 2 * BAND_WIN), lambda h: (h, 0, 0))],
        out_specs=pl.BlockSpec((1, BAND_WIN, BAND_WIN), lambda h: (h, 0, 0)),
        out_shape=jax.ShapeDtypeStruct((N_HEADS, BAND_WIN, BAND_WIN), F32),
        compiler_params=_params("parallel"),
        name="band_bias",
    )(y.reshape(N_HEADS, 1, 2 * BAND_WIN))


def _band_attention_kernel(q_ref, k_ref, v_ref, bias_ref, o_ref, scores_ref):
    n_tiles = q_ref.shape[0] // BAND_TQ

    def window_start(i):
        return max(i * BAND_TQ - LEFT_CHUNKS * CHUNK, 0)

    def scores_into(i, slot):
        t0, start = i * BAND_TQ, window_start(i)
        scores = lax.dot_general(q_ref[t0:t0 + BAND_TQ, :], k_ref[start:start + BAND_WIN, :],
                                 (((1,), (1,)), ((), ())), preferred_element_type=F32)
        scores_ref[slot] = scores + bias_ref[0, t0 - start:t0 - start + BAND_TQ, :]

    def softmax_pv(i, slot):
        t0, start = i * BAND_TQ, window_start(i)
        scores = scores_ref[slot]
        p = jnp.exp2(scores - jnp.max(scores, axis=-1, keepdims=True))
        inv_l = 1.0 / jnp.sum(p, axis=-1, keepdims=True)
        out = jnp.dot(p.astype(BF16), v_ref[start:start + BAND_WIN, :],
                      preferred_element_type=F32) * inv_l
        o_ref[t0:t0 + BAND_TQ, :] = out.astype(o_ref.dtype)

    scores_into(0, 0)
    for i in range(n_tiles):
        if i + 1 < n_tiles:
            scores_into(i + 1, (i + 1) % 2)
        softmax_pv(i, i % 2)


def _band_attention(q, kv, bias, batch, seq):
    head_block = lambda col0: pl.BlockSpec((seq, HEAD_DIM), lambda b, h: (b, col0 + h))
    return pl.pallas_call(
        _band_attention_kernel,
        grid=(batch, N_HEADS),
        in_specs=[head_block(0), head_block(0), head_block(N_HEADS),
                  pl.BlockSpec((1, BAND_WIN, BAND_WIN), lambda b, h: (h, 0, 0))],
        out_specs=head_block(0),
        out_shape=jax.ShapeDtypeStruct((batch * seq, D_MODEL), BF16),
        scratch_shapes=[pltpu.VMEM((2, BAND_TQ, BAND_WIN), F32)],
        compiler_params=_params("parallel", "parallel"),
        name="band_attention",
    )(q, kv, kv, bias)


def kernel(x, g_ffn, w_ffn_gate, w_ffn_up, w_ffn_down, g_mix, w_qkv_a, w_o_a, g_kv,
           w_kv_shared, w_q_b, w_o_b, rel_bias_b, g_final):
    batch, seq, d = x.shape
    depth = g_ffn.shape[0]
    n_a = w_qkv_a.shape[0]

    w_gate, w_up, w_down = w_ffn_gate, w_ffn_up, w_ffn_down

    h = x.reshape(batch * seq, d)
    kv = None
    for layer in range(depth):
        h = _ffn_half_step(h, g_ffn[layer, 0], w_gate, w_up, w_down, (layer, 0))
        if layer < n_a:
            qkv = _norm_matmul(h, g_mix[layer], w_qkv_a, (layer,), query_cols=d)
            mix = _stick_breaking_attention(qkv, batch, seq)
            h = _matmul_residual(mix, w_o_a, h, (layer,))
        else:
            lb = layer - n_a
            if kv is None:
                kv = _norm_matmul(h, g_kv, w_kv_shared)
            q = _norm_matmul(h, g_mix[layer], w_q_b, (lb,), query_cols=d)
            mix = _band_attention(q, kv, _band_bias(rel_bias_b[lb]), batch, seq)
            h = _matmul_residual(mix, w_o_b, h, (lb,))
        h = _ffn_half_step(h, g_ffn[layer, 1], w_gate, w_up, w_down, (layer, 1),
                           g_out=g_final if layer == depth - 1 else None)
    return h.reshape(batch, seq, d)
```

```python
import functools
import math

import jax
import jax.numpy as jnp
from jax import lax
from jax.experimental import pallas as pl
from jax.experimental.pallas import tpu as pltpu

D_MODEL = 2048
N_HEADS = 16
HEAD_DIM = D_MODEL // N_HEADS
CHUNK = 64
LEFT_CHUNKS = 8
REL_CLIP = 256
EPS = 1e-6
LOG2E = math.log2(math.e)
QUERY_SCALE = HEAD_DIM ** -0.5 * LOG2E

BF16 = jnp.bfloat16
F32 = jnp.float32

VMEM_LIMIT_BYTES = 56 * 1024 * 1024

TILE_M = 1024
TILE_N = 1024
TILE_F = 256
FFN_DOWN_COLS = 512
FFN_FINAL_ROWS = 128

SB_TQ = 256
SB_KCHUNK = 2 * SB_TQ
SB_KSTEP = 256
SB_KSUB = 128
SB_DEAD_LOG2 = 160.0

BAND_TQ = 256
BAND_WIN = BAND_TQ + LEFT_CHUNKS * CHUNK
BAND_SHIFTS = LEFT_CHUNKS * CHUNK // BAND_TQ + 1
BIAS_ROWS = 8

assert REL_CLIP <= BAND_WIN and BAND_WIN % 128 == 0


def _params(*semantics):
    return pltpu.CompilerParams(dimension_semantics=semantics,
                                vmem_limit_bytes=VMEM_LIMIT_BYTES)


def _stacked_spec(lead, block, index_map):
    lead = tuple(lead)
    return pl.BlockSpec((None,) * len(lead) + tuple(block),
                        lambda *grid_ids: lead + tuple(index_map(*grid_ids)))


def _rms_norm_f32(x, g):
    y = x * lax.rsqrt(jnp.mean(x * x, axis=-1, keepdims=True) + EPS)
    return y * g


def _norm_matmul_kernel(x_ref, g_ref, w_ref, o_ref, xn_ref, *, query_tiles):
    @pl.when(pl.program_id(1) == 0)
    def _():
        xn_ref[...] = _rms_norm_f32(x_ref[...], g_ref[...]).astype(BF16)

    acc = jnp.dot(xn_ref[...], w_ref[...].astype(BF16), preferred_element_type=F32)
    if query_tiles:
        acc = acc * jnp.where(pl.program_id(1) < query_tiles, QUERY_SCALE, 1.0)
    o_ref[...] = acc.astype(o_ref.dtype)


def _norm_matmul(x, g, w, lead=(), query_cols=0):
    m, d = x.shape
    n = w.shape[-1]
    return pl.pallas_call(
        functools.partial(_norm_matmul_kernel, query_tiles=query_cols // TILE_N),
        grid=(m // TILE_M, n // TILE_N),
        in_specs=[pl.BlockSpec((TILE_M, d), lambda i, j: (i, 0)),
                  pl.BlockSpec((1, d), lambda i, j: (0, 0)),
                  _stacked_spec(lead, (d, TILE_N), lambda i, j: (0, j))],
        out_specs=pl.BlockSpec((TILE_M, TILE_N), lambda i, j: (i, j)),
        out_shape=jax.ShapeDtypeStruct((m, n), BF16),
        scratch_shapes=[pltpu.VMEM((TILE_M, d), BF16)],
        compiler_params=_params("parallel", "arbitrary"),
        name="norm_matmul",
    )(x, g.reshape(1, d), w)


def _matmul_residual_kernel(a_ref, w_ref, r_ref, o_ref, wb_ref):
    @pl.when(pl.program_id(1) == 0)
    def _():
        wb_ref[...] = w_ref[...].astype(BF16)

    o_ref[...] = r_ref[...] + jnp.dot(a_ref[...], wb_ref[...], preferred_element_type=F32)


def _matmul_residual(a, w, res, lead=()):
    m, k = a.shape
    n = w.shape[-1]
    return pl.pallas_call(
        _matmul_residual_kernel,
        grid=(n // TILE_N, m // TILE_M),
        in_specs=[pl.BlockSpec((TILE_M, k), lambda j, i: (i, 0)),
                  _stacked_spec(lead, (k, TILE_N), lambda j, i: (0, j)),
                  pl.BlockSpec((TILE_M, TILE_N), lambda j, i: (i, j))],
        out_specs=pl.BlockSpec((TILE_M, TILE_N), lambda j, i: (i, j)),
        out_shape=jax.ShapeDtypeStruct((m, n), F32),
        scratch_shapes=[pltpu.VMEM((k, TILE_N), BF16)],
        compiler_params=_params("parallel", "arbitrary"),
        name="matmul_residual",
    )(a, w, res)


def _ffn_kernel(h_ref, g_ref, wg_ref, wu_ref, wd_ref, *rest, out_norm):
    g_out_ref, o_ref, xn_ref = rest if out_norm else (None,) + rest
    j = pl.program_id(1)

    @pl.when(j == 0)
    def _():
        xn_ref[...] = _rms_norm_f32(h_ref[...], g_ref[...]).astype(BF16)
        o_ref[...] = jnp.zeros_like(o_ref)

    xn = xn_ref[...]
    gate = jnp.dot(xn, wg_ref[...].astype(BF16), preferred_element_type=F32)
    up = jnp.dot(xn, wu_ref[...].astype(BF16), preferred_element_type=F32)
    hidden = (gate * (1.0 / (1.0 + jnp.exp(-gate))) * up).astype(BF16)
    for n0 in range(0, o_ref.shape[1], FFN_DOWN_COLS):
        cols = slice(n0, n0 + FFN_DOWN_COLS)
        o_ref[:, cols] += jnp.dot(hidden, wd_ref[:, cols].astype(BF16),
                                  preferred_element_type=F32)

    @pl.when(j == pl.num_programs(1) - 1)
    def _():
        def finish_rows(r, carry):
            rows = pl.ds(pl.multiple_of(r * FFN_FINAL_ROWS, FFN_FINAL_ROWS), FFN_FINAL_ROWS)
            y = h_ref[rows, :] + 0.5 * o_ref[rows, :]
            o_ref[rows, :] = _rms_norm_f32(y, g_out_ref[...]) if out_norm else y
            return carry

        lax.fori_loop(0, o_ref.shape[0] // FFN_FINAL_ROWS, finish_rows, 0)


def _ffn_half_step(h, g, w_gate, w_up, w_down, lead, g_out=None):
    m, d = h.shape
    f = w_gate.shape[-1]
    gain_spec = pl.BlockSpec((1, d), lambda i, j: (0, 0))
    out_norm = g_out is not None
    return pl.pallas_call(
        functools.partial(_ffn_kernel, out_norm=out_norm),
        grid=(m // TILE_M, f // TILE_F),
        in_specs=[pl.BlockSpec((TILE_M, d), lambda i, j: (i, 0)),
                  gain_spec,
                  _stacked_spec(lead, (d, TILE_F), lambda i, j: (0, j)),
                  _stacked_spec(lead, (d, TILE_F), lambda i, j: (0, j)),
                  _stacked_spec(lead, (TILE_F, d), lambda i, j: (j, 0))]
        + [gain_spec] * out_norm,
        out_specs=pl.BlockSpec((TILE_M, d), lambda i, j: (i, 0)),
        out_shape=jax.ShapeDtypeStruct((m, d), F32),
        scratch_shapes=[pltpu.VMEM((TILE_M, d), BF16)],
        compiler_params=_params("parallel", "arbitrary"),
        name="ffn_half_step",
    )(h, g.reshape(1, d), w_gate, w_up, w_down, *([g_out.reshape(1, d)] if out_norm else []))


def _sb_key_block(q, k_ref, v_ref, start, width, mask, suffix_mat, acc_ref, csum_ref):
    ks = k_ref[pl.ds(start, width), :]
    vs = v_ref[pl.ds(start, width), :]
    w = lax.dot_general(q, ks, (((1,), (1,)), ((), ())), preferred_element_type=F32)
    yield
    sp = jnp.maximum(w, 0.0) + jnp.log2(1.0 + jnp.exp2(-jnp.abs(w)))
    sp_sum = sp if mask is None else jnp.where(mask, sp, 0.0)
    yield

    csum = csum_ref[...]
    tails = []
    for sub in reversed(range(width // SB_KSUB)):
        blk = sp_sum[:, sub * SB_KSUB:(sub + 1) * SB_KSUB]
        hi = blk.astype(BF16)
        lo = (blk - hi.astype(F32)).astype(BF16)
        sums = jnp.dot(jnp.concatenate([hi, lo], axis=1), suffix_mat,
                       preferred_element_type=F32)
        tails.append(sums[:, :SB_KSUB] + csum)
        csum = csum + sums[:, SB_KSUB:]
    csum_ref[...] = csum
    tail = jnp.concatenate(tails[::-1], axis=1)
    yield

    a = jnp.exp2((w - sp) - tail)
    if mask is not None:
        a = jnp.where(mask, a, 0.0)
    yield
    acc_ref[...] += jnp.dot(a.astype(BF16), vs, preferred_element_type=F32)


def _interleave(*streams):
    streams = list(streams)
    while streams:
        for stream in list(streams):
            if next(stream, StopIteration) is StopIteration:
                streams.remove(stream)


def _stick_breaking_kernel(q_ref, k_ref, v_ref, o_ref, acc_ref, csum_ref):
    seq = q_ref.shape[0]

    row = lax.broadcasted_iota(jnp.int32, (2 * SB_KSUB, 2 * SB_KSUB), 0) & (SB_KSUB - 1)
    col = lax.broadcasted_iota(jnp.int32, (2 * SB_KSUB, 2 * SB_KSUB), 1)
    suffix_mat = jnp.where((col >= SB_KSUB) | (row > col), 1.0, 0.0).astype(BF16)

    def row_pair(p, carry):
        base = pl.multiple_of(p * SB_KCHUNK, SB_KCHUNK)
        acc_ref[...] = jnp.zeros_like(acc_ref)
        csum_ref[...] = jnp.zeros_like(csum_ref)
        row_tiles = [pl.ds(row_offset, SB_TQ) for row_offset in range(0, SB_KCHUNK, SB_TQ)]
        qs = [q_ref[pl.ds(pl.multiple_of(base + rows.start, SB_TQ), SB_TQ), :]
              for rows in row_tiles]

        def diagonal(q, rows):
            width = rows.start + SB_TQ
            r = lax.broadcasted_iota(jnp.int32, (SB_TQ, width), 0)
            c = lax.broadcasted_iota(jnp.int32, (SB_TQ, width), 1)
            return _sb_key_block(q, k_ref, v_ref, base, width, c < r + rows.start, suffix_mat,
                                 acc_ref.at[rows], csum_ref.at[rows])

        _interleave(*(diagonal(q, rows) for q, rows in zip(qs, row_tiles)))
        n_steps = base // SB_KSTEP

        def keys_left(state):
            n, live = state
            return jnp.logical_and(n < n_steps, live)

        def key_step(state):
            n, _ = state
            start = pl.multiple_of(base - (n + 1) * SB_KSTEP, SB_KSTEP)
            _interleave(*(_sb_key_block(q, k_ref, v_ref, start, SB_KSTEP, None, suffix_mat,
                                        acc_ref.at[rows], csum_ref.at[rows])
                          for q, rows in zip(qs, row_tiles)))
            return n + 1, jnp.min(csum_ref[...]) < SB_DEAD_LOG2

        lax.while_loop(keys_left, key_step, (jnp.int32(0), True))
        o_ref[pl.ds(base, SB_KCHUNK), :] = acc_ref[...].astype(o_ref.dtype)
        return carry

    lax.fori_loop(0, seq // SB_KCHUNK, row_pair, 0)


def _stick_breaking_attention(qkv, batch, seq):
    head_block = lambda col0: pl.BlockSpec((seq, HEAD_DIM), lambda b, h: (b, col0 + h))
    return pl.pallas_call(
        _stick_breaking_kernel,
        grid=(batch, N_HEADS),
        in_specs=[head_block(0), head_block(N_HEADS), head_block(2 * N_HEADS)],
        out_specs=head_block(0),
        out_shape=jax.ShapeDtypeStruct((batch * seq, D_MODEL), BF16),
        scratch_shapes=[pltpu.VMEM((SB_KCHUNK, HEAD_DIM), F32),
                        pltpu.VMEM((SB_KCHUNK, SB_KSUB), F32)],
        compiler_params=_params("parallel", "parallel"),
        name="stick_breaking_attention",
    )(qkv, qkv, qkv)


def _band_bias_kernel(y_ref, o_ref):
    y = jnp.broadcast_to(y_ref[0], (BIAS_ROWS, 2 * BAND_WIN))
    c_chunk = lax.shift_right_logical(
        lax.broadcasted_iota(jnp.int32, (BIAS_ROWS, BAND_WIN), 1), CHUNK.bit_length() - 1)

    def rows(g, carry):
        u0 = pl.multiple_of(g * BIAS_ROWS, BIAS_ROWS)
        rolled = pltpu.roll(y, u0, 1, stride=1, stride_axis=0)
        chunk_gap = lax.shift_right_logical(u0, CHUNK.bit_length() - 1) - c_chunk
        valid = (chunk_gap >= 0) & (chunk_gap <= LEFT_CHUNKS)
        o_ref[0, pl.ds(u0, BIAS_ROWS), :] = jnp.where(valid, rolled[:, BAND_WIN:] * LOG2E,
                                                      -jnp.inf)
        return carry

    lax.fori_loop(0, BAND_WIN // BIAS_ROWS, rows, 0, unroll=32)


def _band_bias(rel_bias):
    edge = BAND_WIN - REL_CLIP
    y = jnp.concatenate(
        [jnp.broadcast_to(rel_bias[:, -1:], (N_HEADS, edge)), rel_bias[:, ::-1],
         jnp.broadcast_to(rel_bias[:, :1], (N_HEADS, edge - 1))], axis=1)
    return pl.pallas_call(
        _band_bias_kernel,
        grid=(N_HEADS,),
        in_specs=[pl.BlockSpec((1, 1, 2 * BAND_WIN), lambda h: (h, 0, 0))],
        out_specs=pl.BlockSpec((1, BAND_WIN, BAND_WIN), lambda h: (h, 0, 0)),
        out_shape=jax.ShapeDtypeStruct((N_HEADS, BAND_WIN, BAND_WIN), F32),
        compiler_params=_params("parallel"),
        name="band_bias",
    )(y.reshape(N_HEADS, 1, 2 * BAND_WIN))


def _band_attention_kernel(q_ref, k_ref, v_ref, bias_ref, o_ref, scores_ref):
    n_tiles = q_ref.shape[0] // BAND_TQ

    def window_start(i):
        return max(i * BAND_TQ - LEFT_CHUNKS * CHUNK, 0)

    def scores_into(i, slot):
        t0, start = i * BAND_TQ, window_start(i)
        scores = lax.dot_general(q_ref[t0:t0 + BAND_TQ, :], k_ref[start:start + BAND_WIN, :],
                                 (((1,), (1,)), ((), ())), preferred_element_type=F32)
        scores_ref[slot] = scores + bias_ref[0, t0 - start:t0 - start + BAND_TQ, :]

    def softmax_pv(i, slot):
        t0, start = i * BAND_TQ, window_start(i)
        scores = scores_ref[slot]
        p = jnp.exp2(scores - jnp.max(scores, axis=-1, keepdims=True))
        inv_l = 1.0 / jnp.sum(p, axis=-1, keepdims=True)
        out = jnp.dot(p.astype(BF16), v_ref[start:start + BAND_WIN, :],
                      preferred_element_type=F32) * inv_l
        o_ref[t0:t0 + BAND_TQ, :] = out.astype(o_ref.dtype)

    scores_into(0, 0)
    for i in range(n_tiles):
        if i + 1 < n_tiles:
            scores_into(i + 1, (i + 1) % 2)
        softmax_pv(i, i % 2)


def _band_attention(q, kv, bias, batch, seq):
    head_block = lambda col0: pl.BlockSpec((seq, HEAD_DIM), lambda b, h: (b, col0 + h))
    return pl.pallas_call(
        _band_attention_kernel,
        grid=(batch, N_HEADS),
        in_specs=[head_block(0), head_block(0), head_block(N_HEADS),
                  pl.BlockSpec((1, BAND_WIN, BAND_WIN), lambda b, h: (h, 0, 0))],
        out_specs=head_block(0),
        out_shape=jax.ShapeDtypeStruct((batch * seq, D_MODEL), BF16),
        scratch_shapes=[pltpu.VMEM((2, BAND_TQ, BAND_WIN), F32)],
        compiler_params=_params("parallel", "parallel"),
        name="band_attention",
    )(q, kv, kv, bias)


def kernel(x, g_ffn, w_ffn_gate, w_ffn_up, w_ffn_down, g_mix, w_qkv_a, w_o_a, g_kv,
           w_kv_shared, w_q_b, w_o_b, rel_bias_b, g_final):
    batch, seq, d = x.shape
    depth = g_ffn.shape[0]
    n_a = w_qkv_a.shape[0]

    w_gate, w_up, w_down = w_ffn_gate, w_ffn_up, w_ffn_down

    h = x.reshape(batch * seq, d)
    kv = None
    for layer in range(depth):
        h = _ffn_half_step(h, g_ffn[layer, 0], w_gate, w_up, w_down, (layer, 0))
        if layer < n_a:
            qkv = _norm_matmul(h, g_mix[layer], w_qkv_a, (layer,), query_cols=d)
            mix = _stick_breaking_attention(qkv, batch, seq)
            h = _matmul_residual(mix, w_o_a, h, (layer,))
        else:
            lb = layer - n_a
            if kv is None:
                kv = _norm_matmul(h, g_kv, w_kv_shared)
            q = _norm_matmul(h, g_mix[layer], w_q_b, (lb,), query_cols=d)
            mix = _band_attention(q, kv, _band_bias(rel_bias_b[lb]), batch, seq)
            h = _matmul_residual(mix, w_o_b, h, (lb,))
        h = _ffn_half_step(h, g_ffn[layer, 1], w_gate, w_up, w_down, (layer, 1),
                           g_out=g_final if layer == depth - 1 else None)
    return h.reshape(batch, seq, d)
```

```python
import functools
import math

import jax
import jax.numpy as jnp
from jax import lax
from jax.experimental import pallas as pl
from jax.experimental.pallas import tpu as pltpu

D_MODEL = 2048
N_HEADS = 16
HEAD_DIM = D_MODEL // N_HEADS
CHUNK = 64
LEFT_CHUNKS = 8
REL_CLIP = 256
EPS = 1e-6
LOG2E = math.log2(math.e)
QUERY_SCALE = HEAD_DIM ** -0.5 * LOG2E

BF16 = jnp.bfloat16
F32 = jnp.float32

VMEM_LIMIT_BYTES = 56 * 1024 * 1024

TILE_M = 1024
TILE_N = 1024
TILE_F = 256
FFN_DOWN_COLS = 512
FFN_FINAL_ROWS = 128

SB_TQ = 256
SB_KCHUNK = 2 * SB_TQ
SB_KSTEP = 256
SB_KSUB = 128
SB_DEAD_LOG2 = 160.0

BAND_TQ = 256
BAND_WIN = BAND_TQ + LEFT_CHUNKS * CHUNK
BAND_SHIFTS = LEFT_CHUNKS * CHUNK // BAND_TQ + 1
BIAS_ROWS = 8

assert REL_CLIP <= BAND_WIN and BAND_WIN % 128 == 0


def _params(*semantics):
    return pltpu.CompilerParams(dimension_semantics=semantics,
                                vmem_limit_bytes=VMEM_LIMIT_BYTES)


def _stacked_spec(lead, block, index_map):
    lead = tuple(lead)
    return pl.BlockSpec((None,) * len(lead) + tuple(block),
                        lambda *grid_ids: lead + tuple(index_map(*grid_ids)))


def _rms_norm_f32(x, g):
    y = x * lax.rsqrt(jnp.mean(x * x, axis=-1, keepdims=True) + EPS)
    return y * g


def _serpentine(i, j, n):
    return jnp.where(i % 2 == 0, j, n - 1 - j)


def _norm_matmul_kernel(x_ref, g_ref, w_ref, o_ref, xn_ref, *, query_tiles):
    @pl.when(pl.program_id(1) == 0)
    def _():
        xn_ref[...] = _rms_norm_f32(x_ref[...], g_ref[...]).astype(BF16)

    acc = jnp.dot(xn_ref[...], w_ref[...].astype(BF16), preferred_element_type=F32)
    if query_tiles:
        col_tile = _serpentine(pl.program_id(0), pl.program_id(1), pl.num_programs(1))
        acc = acc * jnp.where(col_tile < query_tiles, QUERY_SCALE, 1.0)
    o_ref[...] = acc.astype(o_ref.dtype)


def _norm_matmul(x, g, w, lead=(), query_cols=0):
    m, d = x.shape
    n = w.shape[-1]
    n_tiles = n // TILE_N
    return pl.pallas_call(
        functools.partial(_norm_matmul_kernel, query_tiles=query_cols // TILE_N),
        grid=(m // TILE_M, n_tiles),
        in_specs=[pl.BlockSpec((TILE_M, d), lambda i, j: (i, 0)),
                  pl.BlockSpec((1, d), lambda i, j: (0, 0)),
                  _stacked_spec(lead, (d, TILE_N), lambda i, j: (0, _serpentine(i, j, n_tiles)))],
        out_specs=pl.BlockSpec((TILE_M, TILE_N), lambda i, j: (i, _serpentine(i, j, n_tiles))),
        out_shape=jax.ShapeDtypeStruct((m, n), BF16),
        scratch_shapes=[pltpu.VMEM((TILE_M, d), BF16)],
        compiler_params=_params("parallel", "arbitrary"),
        name="norm_matmul",
    )(x, g.reshape(1, d), w)


def _matmul_residual_kernel(a_ref, w_ref, r_ref, o_ref, wb_ref):
    @pl.when(pl.program_id(1) == 0)
    def _():
        wb_ref[...] = w_ref[...].astype(BF16)

    o_ref[...] = r_ref[...] + jnp.dot(a_ref[...], wb_ref[...], preferred_element_type=F32)


def _matmul_residual(a, w, res, lead=()):
    m, k = a.shape
    n = w.shape[-1]
    return pl.pallas_call(
        _matmul_residual_kernel,
        grid=(n // TILE_N, m // TILE_M),
        in_specs=[pl.BlockSpec((TILE_M, k), lambda j, i: (i, 0)),
                  _stacked_spec(lead, (k, TILE_N), lambda j, i: (0, j)),
                  pl.BlockSpec((TILE_M, TILE_N), lambda j, i: (i, j))],
        out_specs=pl.BlockSpec((TILE_M, TILE_N), lambda j, i: (i, j)),
        out_shape=jax.ShapeDtypeStruct((m, n), F32),
        scratch_shapes=[pltpu.VMEM((k, TILE_N), BF16)],
        compiler_params=_params("parallel", "arbitrary"),
        name="matmul_residual",
    )(a, w, res)


def _ffn_kernel(h_ref, g_ref, wg_ref, wu_ref, wd_ref, *rest, out_norm):
    g_out_ref, o_ref, xn_ref = rest if out_norm else (None,) + rest
    j = pl.program_id(1)

    @pl.when(j == 0)
    def _():
        xn_ref[...] = _rms_norm_f32(h_ref[...], g_ref[...]).astype(BF16)
        o_ref[...] = jnp.zeros_like(o_ref)

    xn = xn_ref[...]
    gate = jnp.dot(xn, wg_ref[...].astype(BF16), preferred_element_type=F32)
    up = jnp.dot(xn, wu_ref[...].astype(BF16), preferred_element_type=F32)
    hidden = (gate * (1.0 / (1.0 + jnp.exp(-gate))) * up).astype(BF16)
    for n0 in range(0, o_ref.shape[1], FFN_DOWN_COLS):
        cols = slice(n0, n0 + FFN_DOWN_COLS)
        o_ref[:, cols] += jnp.dot(hidden, wd_ref[:, cols].astype(BF16),
                                  preferred_element_type=F32)

    @pl.when(j == pl.num_programs(1) - 1)
    def _():
        def finish_rows(r, carry):
            rows = pl.ds(pl.multiple_of(r * FFN_FINAL_ROWS, FFN_FINAL_ROWS), FFN_FINAL_ROWS)
            y = h_ref[rows, :] + 0.5 * o_ref[rows, :]
            o_ref[rows, :] = _rms_norm_f32(y, g_out_ref[...]) if out_norm else y
            return carry

        lax.fori_loop(0, o_ref.shape[0] // FFN_FINAL_ROWS, finish_rows, 0)


def _ffn_half_step(h, g, w_gate, w_up, w_down, lead, g_out=None):
    m, d = h.shape
    f = w_gate.shape[-1]
    gain_spec = pl.BlockSpec((1, d), lambda i, j: (0, 0))
    out_norm = g_out is not None
    f_tiles = f // TILE_F
    hidden_tile = lambda i, j: _serpentine(i, j, f_tiles)
    return pl.pallas_call(
        functools.partial(_ffn_kernel, out_norm=out_norm),
        grid=(m // TILE_M, f_tiles),
        in_specs=[pl.BlockSpec((TILE_M, d), lambda i, j: (i, 0)),
                  gain_spec,
                  _stacked_spec(lead, (d, TILE_F), lambda i, j: (0, hidden_tile(i, j))),
                  _stacked_spec(lead, (d, TILE_F), lambda i, j: (0, hidden_tile(i, j))),
                  _stacked_spec(lead, (TILE_F, d), lambda i, j: (hidden_tile(i, j), 0))]
        + [gain_spec] * out_norm,
        out_specs=pl.BlockSpec((TILE_M, d), lambda i, j: (i, 0)),
        out_shape=jax.ShapeDtypeStruct((m, d), F32),
        scratch_shapes=[pltpu.VMEM((TILE_M, d), BF16)],
        compiler_params=_params("parallel", "arbitrary"),
        name="ffn_half_step",
    )(h, g.reshape(1, d), w_gate, w_up, w_down, *([g_out.reshape(1, d)] if out_norm else []))


def _sb_key_block(q, k_ref, v_ref, start, width, mask, suffix_mat, acc_ref, csum_ref):
    ks = k_ref[pl.ds(start, width), :]
    vs = v_ref[pl.ds(start, width), :]
    w = lax.dot_general(q, ks, (((1,), (1,)), ((), ())), preferred_element_type=F32)
    yield
    sp = jnp.maximum(w, 0.0) + jnp.log2(1.0 + jnp.exp2(-jnp.abs(w)))
    sp_sum = sp if mask is None else jnp.where(mask, sp, 0.0)
    yield

    csum = csum_ref[...]
    tails = []
    for sub in reversed(range(width // SB_KSUB)):
        blk = sp_sum[:, sub * SB_KSUB:(sub + 1) * SB_KSUB]
        hi = blk.astype(BF16)
        lo = (blk - hi.astype(F32)).astype(BF16)
        sums = jnp.dot(jnp.concatenate([hi, lo], axis=1), suffix_mat,
                       preferred_element_type=F32)
        tails.append(sums[:, :SB_KSUB] + csum)
        csum = csum + sums[:, SB_KSUB:]
    csum_ref[...] = csum
    tail = jnp.concatenate(tails[::-1], axis=1)
    yield

    a = jnp.exp2((w - sp) - tail)
    if mask is not None:
        a = jnp.where(mask, a, 0.0)
    yield
    acc_ref[...] += jnp.dot(a.astype(BF16), vs, preferred_element_type=F32)


def _interleave(*streams):
    streams = list(streams)
    while streams:
        for stream in list(streams):
            if next(stream, StopIteration) is StopIteration:
                streams.remove(stream)


def _stick_breaking_kernel(q_ref, k_ref, v_ref, o_ref, acc_ref, csum_ref):
    seq = q_ref.shape[0]

    row = lax.broadcasted_iota(jnp.int32, (2 * SB_KSUB, 2 * SB_KSUB), 0) & (SB_KSUB - 1)
    col = lax.broadcasted_iota(jnp.int32, (2 * SB_KSUB, 2 * SB_KSUB), 1)
    suffix_mat = jnp.where((col >= SB_KSUB) | (row > col), 1.0, 0.0).astype(BF16)

    def row_pair(p, carry):
        base = pl.multiple_of(p * SB_KCHUNK, SB_KCHUNK)
        acc_ref[...] = jnp.zeros_like(acc_ref)
        csum_ref[...] = jnp.zeros_like(csum_ref)
        row_tiles = [pl.ds(row_offset, SB_TQ) for row_offset in range(0, SB_KCHUNK, SB_TQ)]
        qs = [q_ref[pl.ds(pl.multiple_of(base + rows.start, SB_TQ), SB_TQ), :]
              for rows in row_tiles]

        def diagonal(q, rows):
            width = rows.start + SB_TQ
            r = lax.broadcasted_iota(jnp.int32, (SB_TQ, width), 0)
            c = lax.broadcasted_iota(jnp.int32, (SB_TQ, width), 1)
            return _sb_key_block(q, k_ref, v_ref, base, width, c < r + rows.start, suffix_mat,
                                 acc_ref.at[rows], csum_ref.at[rows])

        _interleave(*(diagonal(q, rows) for q, rows in zip(qs, row_tiles)))
        n_steps = base // SB_KSTEP

        def keys_left(state):
            n, live = state
            return jnp.logical_and(n < n_steps, live)

        def key_step(state):
            n, _ = state
            start = pl.multiple_of(base - (n + 1) * SB_KSTEP, SB_KSTEP)
            _interleave(*(_sb_key_block(q, k_ref, v_ref, start, SB_KSTEP, None, suffix_mat,
                                        acc_ref.at[rows], csum_ref.at[rows])
                          for q, rows in zip(qs, row_tiles)))
            return n + 1, jnp.min(csum_ref[...]) < SB_DEAD_LOG2

        lax.while_loop(keys_left, key_step, (jnp.int32(0), True))
        o_ref[pl.ds(base, SB_KCHUNK), :] = acc_ref[...].astype(o_ref.dtype)
        return carry

    lax.fori_loop(0, seq // SB_KCHUNK, row_pair, 0)


def _stick_breaking_attention(qkv, batch, seq):
    head_block = lambda col0: pl.BlockSpec((seq, HEAD_DIM), lambda b, h: (b, col0 + h))
    return pl.pallas_call(
        _stick_breaking_kernel,
        grid=(batch, N_HEADS),
        in_specs=[head_block(0), head_block(N_HEADS), head_block(2 * N_HEADS)],
        out_specs=head_block(0),
        out_shape=jax.ShapeDtypeStruct((batch * seq, D_MODEL), BF16),
        scratch_shapes=[pltpu.VMEM((SB_KCHUNK, HEAD_DIM), F32),
                        pltpu.VMEM((SB_KCHUNK, SB_KSUB), F32)],
        compiler_params=_params("parallel", "parallel"),
        name="stick_breaking_attention",
    )(qkv, qkv, qkv)


def _band_bias_kernel(y_ref, o_ref):
    y = jnp.broadcast_to(y_ref[0], (BIAS_ROWS, 2 * BAND_WIN))
    c_chunk = lax.shift_right_logical(
        lax.broadcasted_iota(jnp.int32, (BIAS_ROWS, BAND_WIN), 1), CHUNK.bit_length() - 1)

    def rows(g, carry):
        u0 = pl.multiple_of(g * BIAS_ROWS, BIAS_ROWS)
        rolled = pltpu.roll(y, u0, 1, stride=1, stride_axis=0)
        chunk_gap = lax.shift_right_logical(u0, CHUNK.bit_length() - 1) - c_chunk
        valid = (chunk_gap >= 0) & (chunk_gap <= LEFT_CHUNKS)
        o_ref[0, pl.ds(u0, BIAS_ROWS), :] = jnp.where(valid, rolled[:, BAND_WIN:] * LOG2E,
                                                      -jnp.inf)
        return carry

    lax.fori_loop(0, BAND_WIN // BIAS_ROWS, rows, 0, unroll=32)


def _band_bias(rel_bias):
    edge = BAND_WIN - REL_CLIP
    y = jnp.concatenate(
        [jnp.broadcast_to(rel_bias[:, -1:], (N_HEADS, edge)), rel_bias[:, ::-1],
         jnp.broadcast_to(rel_bias[:, :1], (N_HEADS, edge - 1))], axis=1)
    return pl.pallas_call(
        _band_bias_kernel,
        grid=(N_HEADS,),
        in_specs=[pl.BlockSpec((1, 1, 2 * BAND_WIN), lambda h: (h, 0, 0))],
        out_specs=pl.BlockSpec((1, BAND_WIN, BAND_WIN), lambda h: (h, 0, 0)),
        out_shape=jax.ShapeDtypeStruct((N_HEADS, BAND_WIN, BAND_WIN), F32),
        compiler_params=_params("parallel"),
        name="band_bias",
    )(y.reshape(N_HEADS, 1, 2 * BAND_WIN))


def _band_attention_kernel(q_ref, k_ref, v_ref, bias_ref, o_ref, scores_ref):
    n_tiles = q_ref.shape[0] // BAND_TQ

    def window_start(i):
        return max(i * BAND_TQ - LEFT_CHUNKS * CHUNK, 0)

    def scores_into(i, slot):
        t0, start = i * BAND_TQ, window_start(i)
        scores = lax.dot_general(q_ref[t0:t0 + BAND_TQ, :], k_ref[start:start + BAND_WIN, :],
                                 (((1,), (1,)), ((), ())), preferred_element_type=F32)
        scores_ref[slot] = scores + bias_ref[0, t0 - start:t0 - start + BAND_TQ, :]

    def softmax_pv(i, slot):
        t0, start = i * BAND_TQ, window_start(i)
        scores = scores_ref[slot]
        p = jnp.exp2(scores - jnp.max(scores, axis=-1, keepdims=True))
        inv_l = 1.0 / jnp.sum(p, axis=-1, keepdims=True)
        out = jnp.dot(p.astype(BF16), v_ref[start:start + BAND_WIN, :],
                      preferred_element_type=F32) * inv_l
        o_ref[t0:t0 + BAND_TQ, :] = out.astype(o_ref.dtype)

    scores_into(0, 0)
    for i in range(n_tiles):
        if i + 1 < n_tiles:
            scores_into(i + 1, (i + 1) % 2)
        softmax_pv(i, i % 2)


def _band_attention(q, kv, bias, batch, seq):
    head_block = lambda col0: pl.BlockSpec((seq, HEAD_DIM), lambda h, b: (b, col0 + h))
    return pl.pallas_call(
        _band_attention_kernel,
        grid=(N_HEADS, batch),
        in_specs=[head_block(0), head_block(0), head_block(N_HEADS),
                  pl.BlockSpec((1, BAND_WIN, BAND_WIN), lambda h, b: (h, 0, 0))],
        out_specs=head_block(0),
        out_shape=jax.ShapeDtypeStruct((batch * seq, D_MODEL), BF16),
        scratch_shapes=[pltpu.VMEM((2, BAND_TQ, BAND_WIN), F32)],
        compiler_params=_params("parallel", "parallel"),
        name="band_attention",
    )(q, kv, kv, bias)


def kernel(x, g_ffn, w_ffn_gate, w_ffn_up, w_ffn_down, g_mix, w_qkv_a, w_o_a, g_kv,
           w_kv_shared, w_q_b, w_o_b, rel_bias_b, g_final):
    batch, seq, d = x.shape
    depth = g_ffn.shape[0]
    n_a = w_qkv_a.shape[0]

    w_gate, w_up, w_down = w_ffn_gate, w_ffn_up, w_ffn_down

    h = x.reshape(batch * seq, d)
    kv = None
    for layer in range(depth):
        h = _ffn_half_step(h, g_ffn[layer, 0], w_gate, w_up, w_down, (layer, 0))
        if layer < n_a:
            qkv = _norm_matmul(h, g_mix[layer], w_qkv_a, (layer,), query_cols=d)
            mix = _stick_breaking_attention(qkv, batch, seq)
            h = _matmul_residual(mix, w_o_a, h, (layer,))
        else:
            lb = layer - n_a
            if kv is None:
                kv = _norm_matmul(h, g_kv, w_kv_shared)
            q = _norm_matmul(h, g_mix[layer], w_q_b, (lb,), query_cols=d)
            mix = _band_attention(q, kv, _band_bias(rel_bias_b[lb]), batch, seq)
            h = _matmul_residual(mix, w_o_b, h, (lb,))
        h = _ffn_half_step(h, g_ffn[layer, 1], w_gate, w_up, w_down, (layer, 1),
                           g_out=g_final if layer == depth - 1 else None)
    return h.reshape(batch, seq, d)
```

```python
import functools
import math

import jax
import jax.numpy as jnp
from jax import lax
from jax.experimental import pallas as pl
from jax.experimental.pallas import tpu as pltpu

D_MODEL = 2048
N_HEADS = 16
HEAD_DIM = D_MODEL // N_HEADS
CHUNK = 64
LEFT_CHUNKS = 8
REL_CLIP = 256
EPS = 1e-6
LOG2E = math.log2(math.e)
QUERY_SCALE = HEAD_DIM ** -0.5 * LOG2E

BF16 = jnp.bfloat16
F32 = jnp.float32

VMEM_LIMIT_BYTES = 56 * 1024 * 1024

TILE_M = 1024
TILE_N = 1024
TILE_F = 256
FFN_DOWN_COLS = 512
FFN_FINAL_ROWS = 128

SB_TQ = 256
SB_KCHUNK = 2 * SB_TQ
SB_KSTEP = 256
SB_KSUB = 128
SB_DEAD_LOG2 = 160.0

BAND_TQ = 256
BAND_WIN = BAND_TQ + LEFT_CHUNKS * CHUNK
BAND_SHIFTS = LEFT_CHUNKS * CHUNK // BAND_TQ + 1
BIAS_ROWS = 8

assert REL_CLIP <= BAND_WIN and BAND_WIN % 128 == 0


def _params(*semantics):
    return pltpu.CompilerParams(dimension_semantics=semantics,
                                vmem_limit_bytes=VMEM_LIMIT_BYTES)


def _stacked_spec(lead, block, index_map):
    lead = tuple(lead)
    return pl.BlockSpec((None,) * len(lead) + tuple(block),
                        lambda *grid_ids: lead + tuple(index_map(*grid_ids)))


def _rms_norm_f32(x, g):
    y = x * lax.rsqrt(jnp.mean(x * x, axis=-1, keepdims=True) + EPS)
    return y * g


def _serpentine(i, j, n):
    return jnp.where(i % 2 == 0, j, n - 1 - j)


def _norm_matmul_kernel(x_ref, g_ref, w_ref, o_ref, xn_ref, *, query_tiles):
    @pl.when(pl.program_id(1) == 0)
    def _():
        xn_ref[...] = _rms_norm_f32(x_ref[...], g_ref[...]).astype(BF16)

    acc = jnp.dot(xn_ref[...], w_ref[...].astype(BF16), preferred_element_type=F32)
    if query_tiles:
        col_tile = _serpentine(pl.program_id(0), pl.program_id(1), pl.num_programs(1))
        acc = acc * jnp.where(col_tile < query_tiles, QUERY_SCALE, 1.0)
    o_ref[...] = acc.astype(o_ref.dtype)


def _norm_matmul(x, g, w, lead=(), query_cols=0):
    m, d = x.shape
    n = w.shape[-1]
    n_tiles = n // TILE_N
    return pl.pallas_call(
        functools.partial(_norm_matmul_kernel, query_tiles=query_cols // TILE_N),
        grid=(m // TILE_M, n_tiles),
        in_specs=[pl.BlockSpec((TILE_M, d), lambda i, j: (i, 0)),
                  pl.BlockSpec((1, d), lambda i, j: (0, 0)),
                  _stacked_spec(lead, (d, TILE_N), lambda i, j: (0, _serpentine(i, j, n_tiles)))],
        out_specs=pl.BlockSpec((TILE_M, TILE_N), lambda i, j: (i, _serpentine(i, j, n_tiles))),
        out_shape=jax.ShapeDtypeStruct((m, n), BF16),
        scratch_shapes=[pltpu.VMEM((TILE_M, d), BF16)],
        compiler_params=_params("parallel", "arbitrary"),
        name="norm_matmul",
    )(x, g.reshape(1, d), w)


def _matmul_residual_kernel(a_ref, w_ref, r_ref, o_ref, wb_ref):
    @pl.when(pl.program_id(1) == 0)
    def _():
        wb_ref[...] = w_ref[...].astype(BF16)

    o_ref[...] = r_ref[...] + jnp.dot(a_ref[...], wb_ref[...], preferred_element_type=F32)


def _matmul_residual(a, w, res, lead=()):
    m, k = a.shape
    n = w.shape[-1]
    return pl.pallas_call(
        _matmul_residual_kernel,
        grid=(n // TILE_N, m // TILE_M),
        in_specs=[pl.BlockSpec((TILE_M, k), lambda j, i: (i, 0)),
                  _stacked_spec(lead, (k, TILE_N), lambda j, i: (0, j)),
                  pl.BlockSpec((TILE_M, TILE_N), lambda j, i: (i, j))],
        out_specs=pl.BlockSpec((TILE_M, TILE_N), lambda j, i: (i, j)),
        out_shape=jax.ShapeDtypeStruct((m, n), F32),
        scratch_shapes=[pltpu.VMEM((k, TILE_N), BF16)],
        compiler_params=_params("parallel", "arbitrary"),
        name="matmul_residual",
    )(a, w, res)


def _ffn_kernel(h_ref, g_ref, wg_ref, wu_ref, wd_ref, *rest, out_norm):
    g_out_ref, o_ref, xn_ref = rest if out_norm else (None,) + rest
    j = pl.program_id(1)

    @pl.when(j == 0)
    def _():
        xn_ref[...] = _rms_norm_f32(h_ref[...], g_ref[...]).astype(BF16)
        o_ref[...] = jnp.zeros_like(o_ref)

    xn = xn_ref[...]
    gate = jnp.dot(xn, wg_ref[...].astype(BF16), preferred_element_type=F32)
    up = jnp.dot(xn, wu_ref[...].astype(BF16), preferred_element_type=F32)
    hidden = (gate * (1.0 / (1.0 + jnp.exp(-gate))) * up).astype(BF16)
    for n0 in range(0, o_ref.shape[1], FFN_DOWN_COLS):
        cols = slice(n0, n0 + FFN_DOWN_COLS)
        o_ref[:, cols] += jnp.dot(hidden, wd_ref[:, cols].astype(BF16),
                                  preferred_element_type=F32)

    @pl.when(j == pl.num_programs(1) - 1)
    def _():
        def finish_rows(r, carry):
            rows = pl.ds(pl.multiple_of(r * FFN_FINAL_ROWS, FFN_FINAL_ROWS), FFN_FINAL_ROWS)
            y = h_ref[rows, :] + 0.5 * o_ref[rows, :]
            o_ref[rows, :] = _rms_norm_f32(y, g_out_ref[...]) if out_norm else y
            return carry

        lax.fori_loop(0, o_ref.shape[0] // FFN_FINAL_ROWS, finish_rows, 0)


def _ffn_half_step(h, g, w_gate, w_up, w_down, lead, g_out=None):
    m, d = h.shape
    f = w_gate.shape[-1]
    gain_spec = pl.BlockSpec((1, d), lambda i, j: (0, 0))
    out_norm = g_out is not None
    return pl.pallas_call(
        functools.partial(_ffn_kernel, out_norm=out_norm),
        grid=(m // TILE_M, f // TILE_F),
        in_specs=[pl.BlockSpec((TILE_M, d), lambda i, j: (i, 0)),
                  gain_spec,
                  _stacked_spec(lead, (d, TILE_F), lambda i, j: (0, j)),
                  _stacked_spec(lead, (d, TILE_F), lambda i, j: (0, j)),
                  _stacked_spec(lead, (TILE_F, d), lambda i, j: (j, 0))]
        + [gain_spec] * out_norm,
        out_specs=pl.BlockSpec((TILE_M, d), lambda i, j: (i, 0)),
        out_shape=jax.ShapeDtypeStruct((m, d), F32),
        scratch_shapes=[pltpu.VMEM((TILE_M, d), BF16)],
        compiler_params=_params("parallel", "arbitrary"),
        name="ffn_half_step",
    )(h, g.reshape(1, d), w_gate, w_up, w_down, *([g_out.reshape(1, d)] if out_norm else []))


def _sb_key_block(q, k_ref, v_ref, start, width, mask, suffix_mat, acc_ref, csum_ref):
    ks = k_ref[pl.ds(start, width), :]
    vs = v_ref[pl.ds(start, width), :]
    w = lax.dot_general(q, ks, (((1,), (1,)), ((), ())), preferred_element_type=F32)
    yield
    sp = jnp.maximum(w, 0.0) + jnp.log2(1.0 + jnp.exp2(-jnp.abs(w)))
    sp_sum = sp if mask is None else jnp.where(mask, sp, 0.0)
    yield

    csum = csum_ref[...]
    tails = []
    for sub in reversed(range(width // SB_KSUB)):
        blk = sp_sum[:, sub * SB_KSUB:(sub + 1) * SB_KSUB]
        hi = blk.astype(BF16)
        lo = (blk - hi.astype(F32)).astype(BF16)
        sums = jnp.dot(jnp.concatenate([hi, lo], axis=1), suffix_mat,
                       preferred_element_type=F32)
        tails.append(sums[:, :SB_KSUB] + csum)
        csum = csum + sums[:, SB_KSUB:]
    csum_ref[...] = csum
    tail = jnp.concatenate(tails[::-1], axis=1)
    yield

    a = jnp.exp2((w - sp) - tail)
    if mask is not None:
        a = jnp.where(mask, a, 0.0)
    yield
    acc_ref[...] += jnp.dot(a.astype(BF16), vs, preferred_element_type=F32)


def _interleave(*streams):
    streams = list(streams)
    while streams:
        for stream in list(streams):
            if next(stream, StopIteration) is StopIteration:
                streams.remove(stream)


def _stick_breaking_kernel(q_ref, k_ref, v_ref, o_ref, acc_ref, csum_ref):
    seq = q_ref.shape[0]

    row = lax.broadcasted_iota(jnp.int32, (2 * SB_KSUB, 2 * SB_KSUB), 0) & (SB_KSUB - 1)
    col = lax.broadcasted_iota(jnp.int32, (2 * SB_KSUB, 2 * SB_KSUB), 1)
    suffix_mat = jnp.where((col >= SB_KSUB) | (row > col), 1.0, 0.0).astype(BF16)

    def row_pair(p, carry):
        base = pl.multiple_of(p * SB_KCHUNK, SB_KCHUNK)
        acc_ref[...] = jnp.zeros_like(acc_ref)
        csum_ref[...] = jnp.zeros_like(csum_ref)
        row_tiles = [pl.ds(row_offset, SB_TQ) for row_offset in range(0, SB_KCHUNK, SB_TQ)]
        qs = [q_ref[pl.ds(pl.multiple_of(base + rows.start, SB_TQ), SB_TQ), :]
              for rows in row_tiles]

        def diagonal(q, rows):
            width = rows.start + SB_TQ
            r = lax.broadcasted_iota(jnp.int32, (SB_TQ, width), 0)
            c = lax.broadcasted_iota(jnp.int32, (SB_TQ, width), 1)
            return _sb_key_block(q, k_ref, v_ref, base, width, c < r + rows.start, suffix_mat,
                                 acc_ref.at[rows], csum_ref.at[rows])

        _interleave(*(diagonal(q, rows) for q, rows in zip(qs, row_tiles)))
        n_steps = base // SB_KSTEP

        def keys_left(state):
            n, live = state
            return jnp.logical_and(n < n_steps, live)

        def key_step(state):
            n, _ = state
            start = pl.multiple_of(base - (n + 1) * SB_KSTEP, SB_KSTEP)
            _interleave(*(_sb_key_block(q, k_ref, v_ref, start, SB_KSTEP, None, suffix_mat,
                                        acc_ref.at[rows], csum_ref.at[rows])
                          for q, rows in zip(qs, row_tiles)))
            return n + 1, jnp.min(csum_ref[...]) < SB_DEAD_LOG2

        lax.while_loop(keys_left, key_step, (jnp.int32(0), True))
        o_ref[pl.ds(base, SB_KCHUNK), :] = acc_ref[...].astype(o_ref.dtype)
        return carry

    lax.fori_loop(0, seq // SB_KCHUNK, row_pair, 0)


def _stick_breaking_attention(qkv, batch, seq):
    head_block = lambda col0: pl.BlockSpec((seq, HEAD_DIM), lambda b, h: (b, col0 + h))
    return pl.pallas_call(
        _stick_breaking_kernel,
        grid=(batch, N_HEADS),
        in_specs=[head_block(0), head_block(N_HEADS), head_block(2 * N_HEADS)],
        out_specs=head_block(0),
        out_shape=jax.ShapeDtypeStruct((batch * seq, D_MODEL), BF16),
        scratch_shapes=[pltpu.VMEM((SB_KCHUNK, HEAD_DIM), F32),
                        pltpu.VMEM((SB_KCHUNK, SB_KSUB), F32)],
        compiler_params=_params("parallel", "parallel"),
        name="stick_breaking_attention",
    )(qkv, qkv, qkv)


def _band_bias_kernel(y_ref, o_ref):
    y = jnp.broadcast_to(y_ref[0], (BIAS_ROWS, 2 * BAND_WIN))
    c_chunk = lax.shift_right_logical(
        lax.broadcasted_iota(jnp.int32, (BIAS_ROWS, BAND_WIN), 1), CHUNK.bit_length() - 1)

    def rows(g, carry):
        u0 = pl.multiple_of(g * BIAS_ROWS, BIAS_ROWS)
        rolled = pltpu.roll(y, u0, 1, stride=1, stride_axis=0)
        chunk_gap = lax.shift_right_logical(u0, CHUNK.bit_length() - 1) - c_chunk
        valid = (chunk_gap >= 0) & (chunk_gap <= LEFT_CHUNKS)
        o_ref[0, pl.ds(u0, BIAS_ROWS), :] = jnp.where(valid, rolled[:, BAND_WIN:] * LOG2E,
                                                      -jnp.inf)
        return carry

    lax.fori_loop(0, BAND_WIN // BIAS_ROWS, rows, 0, unroll=32)


def _band_bias(rel_bias):
    edge = BAND_WIN - REL_CLIP
    y = jnp.concatenate(
        [jnp.broadcast_to(rel_bias[:, -1:], (N_HEADS, edge)), rel_bias[:, ::-1],
         jnp.broadcast_to(rel_bias[:, :1], (N_HEADS, edge - 1))], axis=1)
    return pl.pallas_call(
        _band_bias_kernel,
        grid=(N_HEADS,),
        in_specs=[pl.BlockSpec((1, 1, 2 * BAND_WIN), lambda h: (h, 0, 0))],
        out_specs=pl.BlockSpec((1, BAND_WIN, BAND_WIN), lambda h: (h, 0, 0)),
        out_shape=jax.ShapeDtypeStruct((N_HEADS, BAND_WIN, BAND_WIN), F32),
        compiler_params=_params("parallel"),
        name="band_bias",
    )(y.reshape(N_HEADS, 1, 2 * BAND_WIN))


def _band_attention_kernel(q_ref, k_ref, v_ref, bias_ref, o_ref, scores_ref):
    n_tiles = q_ref.shape[0] // BAND_TQ

    def window_start(i):
        return max(i * BAND_TQ - LEFT_CHUNKS * CHUNK, 0)

    def scores_into(i, slot):
        t0, start = i * BAND_TQ, window_start(i)
        scores = lax.dot_general(q_ref[t0:t0 + BAND_TQ, :], k_ref[start:start + BAND_WIN, :],
                                 (((1,), (1,)), ((), ())), preferred_element_type=F32)
        scores_ref[slot] = scores + bias_ref[0, t0 - start:t0 - start + BAND_TQ, :]

    def softmax_pv(i, slot):
        t0, start = i * BAND_TQ, window_start(i)
        scores = scores_ref[slot]
        p = jnp.exp2(scores - jnp.max(scores, axis=-1, keepdims=True))
        inv_l = 1.0 / jnp.sum(p, axis=-1, keepdims=True)
        out = jnp.dot(p.astype(BF16), v_ref[start:start + BAND_WIN, :],
                      preferred_element_type=F32) * inv_l
        o_ref[t0:t0 + BAND_TQ, :] = out.astype(o_ref.dtype)

    scores_into(0, 0)
    for i in range(n_tiles):
        if i + 1 < n_tiles:
            scores_into(i + 1, (i + 1) % 2)
        softmax_pv(i, i % 2)


def _band_attention(q, kv, bias, batch, seq):
    head_block = lambda col0: pl.BlockSpec((seq, HEAD_DIM), lambda h, b: (b, col0 + h))
    return pl.pallas_call(
        _band_attention_kernel,
        grid=(N_HEADS, batch),
        in_specs=[head_block(0), head_block(0), head_block(N_HEADS),
                  pl.BlockSpec((1, BAND_WIN, BAND_WIN), lambda h, b: (h, 0, 0))],
        out_specs=head_block(0),
        out_shape=jax.ShapeDtypeStruct((batch * seq, D_MODEL), BF16),
        scratch_shapes=[pltpu.VMEM((2, BAND_TQ, BAND_WIN), F32)],
        compiler_params=_params("parallel", "parallel"),
        name="band_attention",
    )(q, kv, kv, bias)


def kernel(x, g_ffn, w_ffn_gate, w_ffn_up, w_ffn_down, g_mix, w_qkv_a, w_o_a, g_kv,
           w_kv_shared, w_q_b, w_o_b, rel_bias_b, g_final):
    batch, seq, d = x.shape
    depth = g_ffn.shape[0]
    n_a = w_qkv_a.shape[0]

    w_gate, w_up, w_down = w_ffn_gate, w_ffn_up, w_ffn_down

    h = x.reshape(batch * seq, d)
    kv = None
    for layer in range(depth):
        h = _ffn_half_step(h, g_ffn[layer, 0], w_gate, w_up, w_down, (layer, 0))
        if layer < n_a:
            qkv = _norm_matmul(h, g_mix[layer], w_qkv_a, (layer,), query_cols=d)
            mix = _stick_breaking_attention(qkv, batch, seq)
            h = _matmul_residual(mix, w_o_a, h, (layer,))
        else:
            lb = layer - n_a
            if kv is None:
                kv = _norm_matmul(h, g_kv, w_kv_shared)
            q = _norm_matmul(h, g_mix[layer], w_q_b, (lb,), query_cols=d)
            mix = _band_attention(q, kv, _band_bias(rel_bias_b[lb]), batch, seq)
            h = _matmul_residual(mix, w_o_b, h, (lb,))
        h = _ffn_half_step(h, g_ffn[layer, 1], w_gate, w_up, w_down, (layer, 1),
                           g_out=g_final if layer == depth - 1 else None)
    return h.reshape(batch, seq, d)
```

```python
import functools
import math

import jax
import jax.numpy as jnp
from jax import lax
from jax.experimental import pallas as pl
from jax.experimental.pallas import tpu as pltpu

D_MODEL = 2048
N_HEADS = 16
HEAD_DIM = D_MODEL // N_HEADS
CHUNK = 64
LEFT_CHUNKS = 8
REL_CLIP = 256
EPS = 1e-6
LOG2E = math.log2(math.e)
QUERY_SCALE = HEAD_DIM ** -0.5 * LOG2E

BF16 = jnp.bfloat16
F32 = jnp.float32

VMEM_LIMIT_BYTES = 56 * 1024 * 1024

TILE_M = 1024
TILE_N = 1024
TILE_F = 256
FFN_DOWN_COLS = 512
FFN_FINAL_ROWS = 128

SB_TQ = 256
SB_KCHUNK = 2 * SB_TQ
SB_KSTEP = 256
SB_KSUB = 128
SB_DEAD_LOG2 = 160.0
SB_HEADS_PER_STEP = 2

BAND_TQ = 256
BAND_WIN = BAND_TQ + LEFT_CHUNKS * CHUNK
BAND_SHIFTS = LEFT_CHUNKS * CHUNK // BAND_TQ + 1
BIAS_ROWS = 8

assert REL_CLIP <= BAND_WIN and BAND_WIN % 128 == 0


def _params(*semantics):
    return pltpu.CompilerParams(dimension_semantics=semantics,
                                vmem_limit_bytes=VMEM_LIMIT_BYTES)


def _stacked_spec(lead, block, index_map):
    lead = tuple(lead)
    return pl.BlockSpec((None,) * len(lead) + tuple(block),
                        lambda *grid_ids: lead + tuple(index_map(*grid_ids)))


def _rms_norm_f32(x, g):
    y = x * lax.rsqrt(jnp.mean(x * x, axis=-1, keepdims=True) + EPS)
    return y * g


def _serpentine(i, j, n):
    return jnp.where(i % 2 == 0, j, n - 1 - j)


def _norm_matmul_kernel(x_ref, g_ref, w_ref, o_ref, xn_ref, *, query_tiles):
    @pl.when(pl.program_id(1) == 0)
    def _():
        xn_ref[...] = _rms_norm_f32(x_ref[...], g_ref[...]).astype(BF16)

    acc = jnp.dot(xn_ref[...], w_ref[...].astype(BF16), preferred_element_type=F32)
    if query_tiles:
        col_tile = _serpentine(pl.program_id(0), pl.program_id(1), pl.num_programs(1))
        acc = acc * jnp.where(col_tile < query_tiles, QUERY_SCALE, 1.0)
    o_ref[...] = acc.astype(o_ref.dtype)


def _norm_matmul(x, g, w, lead=(), query_cols=0):
    m, d = x.shape
    n = w.shape[-1]
    n_tiles = n // TILE_N
    return pl.pallas_call(
        functools.partial(_norm_matmul_kernel, query_tiles=query_cols // TILE_N),
        grid=(m // TILE_M, n_tiles),
        in_specs=[pl.BlockSpec((TILE_M, d), lambda i, j: (i, 0)),
                  pl.BlockSpec((1, d), lambda i, j: (0, 0)),
                  _stacked_spec(lead, (d, TILE_N), lambda i, j: (0, _serpentine(i, j, n_tiles)))],
        out_specs=pl.BlockSpec((TILE_M, TILE_N), lambda i, j: (i, _serpentine(i, j, n_tiles))),
        out_shape=jax.ShapeDtypeStruct((m, n), BF16),
        scratch_shapes=[pltpu.VMEM((TILE_M, d), BF16)],
        compiler_params=_params("parallel", "arbitrary"),
        name="norm_matmul",
    )(x, g.reshape(1, d), w)


def _matmul_residual_kernel(a_ref, w_ref, r_ref, o_ref, wb_ref):
    @pl.when(pl.program_id(1) == 0)
    def _():
        wb_ref[...] = w_ref[...].astype(BF16)

    o_ref[...] = r_ref[...] + jnp.dot(a_ref[...], wb_ref[...], preferred_element_type=F32)


def _matmul_residual(a, w, res, lead=()):
    m, k = a.shape
    n = w.shape[-1]
    return pl.pallas_call(
        _matmul_residual_kernel,
        grid=(n // TILE_N, m // TILE_M),
        in_specs=[pl.BlockSpec((TILE_M, k), lambda j, i: (i, 0)),
                  _stacked_spec(lead, (k, TILE_N), lambda j, i: (0, j)),
                  pl.BlockSpec((TILE_M, TILE_N), lambda j, i: (i, j))],
        out_specs=pl.BlockSpec((TILE_M, TILE_N), lambda j, i: (i, j)),
        out_shape=jax.ShapeDtypeStruct((m, n), F32),
        scratch_shapes=[pltpu.VMEM((k, TILE_N), BF16)],
        compiler_params=_params("parallel", "arbitrary"),
        name="matmul_residual",
    )(a, w, res)


def _ffn_kernel(h_ref, g_ref, wg_ref, wu_ref, wd_ref, *rest, out_norm):
    g_out_ref, o_ref, xn_ref = rest if out_norm else (None,) + rest
    j = pl.program_id(1)

    @pl.when(j == 0)
    def _():
        xn_ref[...] = _rms_norm_f32(h_ref[...], g_ref[...]).astype(BF16)
        o_ref[...] = jnp.zeros_like(o_ref)

    xn = xn_ref[...]
    gate = jnp.dot(xn, wg_ref[...].astype(BF16), preferred_element_type=F32)
    up = jnp.dot(xn, wu_ref[...].astype(BF16), preferred_element_type=F32)
    hidden = (gate * (1.0 / (1.0 + jnp.exp(-gate))) * up).astype(BF16)
    for n0 in range(0, o_ref.shape[1], FFN_DOWN_COLS):
        cols = slice(n0, n0 + FFN_DOWN_COLS)
        o_ref[:, cols] += jnp.dot(hidden, wd_ref[:, cols].astype(BF16),
                                  preferred_element_type=F32)

    @pl.when(j == pl.num_programs(1) - 1)
    def _():
        def finish_rows(r, carry):
            rows = pl.ds(pl.multiple_of(r * FFN_FINAL_ROWS, FFN_FINAL_ROWS), FFN_FINAL_ROWS)
            y = h_ref[rows, :] + 0.5 * o_ref[rows, :]
            o_ref[rows, :] = _rms_norm_f32(y, g_out_ref[...]) if out_norm else y
            return carry

        lax.fori_loop(0, o_ref.shape[0] // FFN_FINAL_ROWS, finish_rows, 0)


def _ffn_half_step(h, g, w_gate, w_up, w_down, lead, g_out=None):
    m, d = h.shape
    f = w_gate.shape[-1]
    gain_spec = pl.BlockSpec((1, d), lambda i, j: (0, 0))
    out_norm = g_out is not None
    return pl.pallas_call(
        functools.partial(_ffn_kernel, out_norm=out_norm),
        grid=(m // TILE_M, f // TILE_F),
        in_specs=[pl.BlockSpec((TILE_M, d), lambda i, j: (i, 0)),
                  gain_spec,
                  _stacked_spec(lead, (d, TILE_F), lambda i, j: (0, j)),
                  _stacked_spec(lead, (d, TILE_F), lambda i, j: (0, j)),
                  _stacked_spec(lead, (TILE_F, d), lambda i, j: (j, 0))]
        + [gain_spec] * out_norm,
        out_specs=pl.BlockSpec((TILE_M, d), lambda i, j: (i, 0)),
        out_shape=jax.ShapeDtypeStruct((m, d), F32),
        scratch_shapes=[pltpu.VMEM((TILE_M, d), BF16)],
        compiler_params=_params("parallel", "arbitrary"),
        name="ffn_half_step",
    )(h, g.reshape(1, d), w_gate, w_up, w_down, *([g_out.reshape(1, d)] if out_norm else []))


def _sb_key_block(q, k_ref, v_ref, start, width, mask, suffix_mat, acc_ref, csum_ref):
    ks = k_ref[pl.ds(start, width), :]
    vs = v_ref[pl.ds(start, width), :]
    w = lax.dot_general(q, ks, (((1,), (1,)), ((), ())), preferred_element_type=F32)
    yield
    sp = jnp.maximum(w, 0.0) + jnp.log2(1.0 + jnp.exp2(-jnp.abs(w)))
    sp_sum = sp if mask is None else jnp.where(mask, sp, 0.0)
    yield

    csum = csum_ref[...]
    tails = []
    for sub in reversed(range(width // SB_KSUB)):
        blk = sp_sum[:, sub * SB_KSUB:(sub + 1) * SB_KSUB]
        hi = blk.astype(BF16)
        lo = (blk - hi.astype(F32)).astype(BF16)
        sums = jnp.dot(jnp.concatenate([hi, lo], axis=1), suffix_mat,
                       preferred_element_type=F32)
        tails.append(sums[:, :SB_KSUB] + csum)
        csum = csum + sums[:, SB_KSUB:]
    csum_ref[...] = csum
    tail = jnp.concatenate(tails[::-1], axis=1)
    yield

    a = jnp.exp2((w - sp) - tail)
    if mask is not None:
        a = jnp.where(mask, a, 0.0)
    yield
    acc_ref[...] += jnp.dot(a.astype(BF16), vs, preferred_element_type=F32)


def _interleave(*streams):
    streams = list(streams)
    while streams:
        for stream in list(streams):
            if next(stream, StopIteration) is StopIteration:
                streams.remove(stream)


def _stick_breaking_kernel(q_ref, k_ref, v_ref, o_ref, acc_ref, csum_ref):
    for head in range(SB_HEADS_PER_STEP):
        cols = pl.ds(head * HEAD_DIM, HEAD_DIM)
        _stick_breaking_head(q_ref.at[:, cols], k_ref.at[:, cols], v_ref.at[:, cols],
                             o_ref.at[:, cols], acc_ref, csum_ref)


def _stick_breaking_head(q_ref, k_ref, v_ref, o_ref, acc_ref, csum_ref):
    seq = q_ref.shape[0]

    row = lax.broadcasted_iota(jnp.int32, (2 * SB_KSUB, 2 * SB_KSUB), 0) & (SB_KSUB - 1)
    col = lax.broadcasted_iota(jnp.int32, (2 * SB_KSUB, 2 * SB_KSUB), 1)
    suffix_mat = jnp.where((col >= SB_KSUB) | (row > col), 1.0, 0.0).astype(BF16)

    def row_pair(p, carry):
        base = pl.multiple_of(p * SB_KCHUNK, SB_KCHUNK)
        acc_ref[...] = jnp.zeros_like(acc_ref)
        csum_ref[...] = jnp.zeros_like(csum_ref)
        row_tiles = [pl.ds(row_offset, SB_TQ) for row_offset in range(0, SB_KCHUNK, SB_TQ)]
        qs = [q_ref[pl.ds(pl.multiple_of(base + rows.start, SB_TQ), SB_TQ), :]
              for rows in row_tiles]

        def diagonal(q, rows):
            width = rows.start + SB_TQ
            r = lax.broadcasted_iota(jnp.int32, (SB_TQ, width), 0)
            c = lax.broadcasted_iota(jnp.int32, (SB_TQ, width), 1)
            return _sb_key_block(q, k_ref, v_ref, base, width, c < r + rows.start, suffix_mat,
                                 acc_ref.at[rows], csum_ref.at[rows])

        _interleave(*(diagonal(q, rows) for q, rows in zip(qs, row_tiles)))
        n_steps = base // SB_KSTEP

        def keys_left(state):
            n, live = state
            return jnp.logical_and(n < n_steps, live)

        def key_step(state):
            n, _ = state
            start = pl.multiple_of(base - (n + 1) * SB_KSTEP, SB_KSTEP)
            _interleave(*(_sb_key_block(q, k_ref, v_ref, start, SB_KSTEP, None, suffix_mat,
                                        acc_ref.at[rows], csum_ref.at[rows])
                          for q, rows in zip(qs, row_tiles)))
            return n + 1, jnp.min(csum_ref[...]) < SB_DEAD_LOG2

        lax.while_loop(keys_left, key_step, (jnp.int32(0), True))
        o_ref[pl.ds(base, SB_KCHUNK), :] = acc_ref[...].astype(o_ref.dtype)
        return carry

    lax.fori_loop(0, seq // SB_KCHUNK, row_pair, 0)


def _stick_breaking_attention(qkv, batch, seq):
    groups = N_HEADS // SB_HEADS_PER_STEP
    head_block = lambda group0: pl.BlockSpec((seq, SB_HEADS_PER_STEP * HEAD_DIM),
                                             lambda b, g: (b, group0 + g))
    return pl.pallas_call(
        _stick_breaking_kernel,
        grid=(batch, groups),
        in_specs=[head_block(0), head_block(groups), head_block(2 * groups)],
        out_specs=head_block(0),
        out_shape=jax.ShapeDtypeStruct((batch * seq, D_MODEL), BF16),
        scratch_shapes=[pltpu.VMEM((SB_KCHUNK, HEAD_DIM), F32),
                        pltpu.VMEM((SB_KCHUNK, SB_KSUB), F32)],
        compiler_params=_params("parallel", "parallel"),
        name="stick_breaking_attention",
    )(qkv, qkv, qkv)


def _band_bias_kernel(y_ref, o_ref):
    y = jnp.broadcast_to(y_ref[0], (BIAS_ROWS, 2 * BAND_WIN))
    c_chunk = lax.shift_right_logical(
        lax.broadcasted_iota(jnp.int32, (BIAS_ROWS, BAND_WIN), 1), CHUNK.bit_length() - 1)

    def rows(g, carry):
        u0 = pl.multiple_of(g * BIAS_ROWS, BIAS_ROWS)
        rolled = pltpu.roll(y, u0, 1, stride=1, stride_axis=0)
        chunk_gap = lax.shift_right_logical(u0, CHUNK.bit_length() - 1) - c_chunk
        valid = (chunk_gap >= 0) & (chunk_gap <= LEFT_CHUNKS)
        o_ref[0, pl.ds(u0, BIAS_ROWS), :] = jnp.where(valid, rolled[:, BAND_WIN:] * LOG2E,
                                                      -jnp.inf)
        return carry

    lax.fori_loop(0, BAND_WIN // BIAS_ROWS, rows, 0, unroll=32)


def _band_bias(rel_bias):
    edge = BAND_WIN - REL_CLIP
    y = jnp.concatenate(
        [jnp.broadcast_to(rel_bias[:, -1:], (N_HEADS, edge)), rel_bias[:, ::-1],
         jnp.broadcast_to(rel_bias[:, :1], (N_HEADS, edge - 1))], axis=1)
    return pl.pallas_call(
        _band_bias_kernel,
        grid=(N_HEADS,),
        in_specs=[pl.BlockSpec((1, 1, 2 * BAND_WIN), lambda h: (h, 0, 0))],
        out_specs=pl.BlockSpec((1, BAND_WIN, BAND_WIN), lambda h: (h, 0, 0)),
        out_shape=jax.ShapeDtypeStruct((N_HEADS, BAND_WIN, BAND_WIN), F32),
        compiler_params=_params("parallel"),
        name="band_bias",
    )(y.reshape(N_HEADS, 1, 2 * BAND_WIN))


def _band_attention_kernel(q_ref, k_ref, v_ref, bias_ref, o_ref, scores_ref):
    n_tiles = q_ref.shape[0] // BAND_TQ

    def window_start(i):
        return max(i * BAND_TQ - LEFT_CHUNKS * CHUNK, 0)

    def scores_into(i, slot):
        t0, start = i * BAND_TQ, window_start(i)
        scores = lax.dot_general(q_ref[t0:t0 + BAND_TQ, :], k_ref[start:start + BAND_WIN, :],
                                 (((1,), (1,)), ((), ())), preferred_element_type=F32)
        scores_ref[slot] = scores + bias_ref[0, t0 - start:t0 - start + BAND_TQ, :]

    def softmax_pv(i, slot):
        t0, start = i * BAND_TQ, window_start(i)
        scores = scores_ref[slot]
        p = jnp.exp2(scores - jnp.max(scores, axis=-1, keepdims=True))
        inv_l = 1.0 / jnp.sum(p, axis=-1, keepdims=True)
        out = jnp.dot(p.astype(BF16), v_ref[start:start + BAND_WIN, :],
                      preferred_element_type=F32) * inv_l
        o_ref[t0:t0 + BAND_TQ, :] = out.astype(o_ref.dtype)

    scores_into(0, 0)
    for i in range(n_tiles):
        if i + 1 < n_tiles:
            scores_into(i + 1, (i + 1) % 2)
        softmax_pv(i, i % 2)


def _band_attention(q, kv, bias, batch, seq):
    head_block = lambda col0: pl.BlockSpec((seq, HEAD_DIM), lambda h, b: (b, col0 + h))
    return pl.pallas_call(
        _band_attention_kernel,
        grid=(N_HEADS, batch),
        in_specs=[head_block(0), head_block(0), head_block(N_HEADS),
                  pl.BlockSpec((1, BAND_WIN, BAND_WIN), lambda h, b: (h, 0, 0))],
        out_specs=head_block(0),
        out_shape=jax.ShapeDtypeStruct((batch * seq, D_MODEL), BF16),
        scratch_shapes=[pltpu.VMEM((2, BAND_TQ, BAND_WIN), F32)],
        compiler_params=_params("parallel", "parallel"),
        name="band_attention",
    )(q, kv, kv, bias)


def kernel(x, g_ffn, w_ffn_gate, w_ffn_up, w_ffn_down, g_mix, w_qkv_a, w_o_a, g_kv,
           w_kv_shared, w_q_b, w_o_b, rel_bias_b, g_final):
    batch, seq, d = x.shape
    depth = g_ffn.shape[0]
    n_a = w_qkv_a.shape[0]

    w_gate, w_up, w_down = w_ffn_gate, w_ffn_up, w_ffn_down

    h = x.reshape(batch * seq, d)
    kv = None
    for layer in range(depth):
        h = _ffn_half_step(h, g_ffn[layer, 0], w_gate, w_up, w_down, (layer, 0))
        if layer < n_a:
            qkv = _norm_matmul(h, g_mix[layer], w_qkv_a, (layer,), query_cols=d)
            mix = _stick_breaking_attention(qkv, batch, seq)
            h = _matmul_residual(mix, w_o_a, h, (layer,))
        else:
            lb = layer - n_a
            if kv is None:
                kv = _norm_matmul(h, g_kv, w_kv_shared)
            q = _norm_matmul(h, g_mix[layer], w_q_b, (lb,), query_cols=d)
            mix = _band_attention(q, kv, _band_bias(rel_bias_b[lb]), batch, seq)
            h = _matmul_residual(mix, w_o_b, h, (lb,))
        h = _ffn_half_step(h, g_ffn[layer, 1], w_gate, w_up, w_down, (layer, 1),
                           g_out=g_final if layer == depth - 1 else None)
    return h.reshape(batch, seq, d)
```
